```python
import math
import jax, jax.numpy as jnp
from jax import lax
import numpy as np

D_MODEL = 1024
BATCH = 8
SEQ = 2048
DEPTH = 1
DEC_BATCH = 128
DEC_SEQ = 8
PAST_LEN = 16384
PAGE_SIZE = 128

HA_HEADS = 4
HA_DK = 128
HA_DV = 128
HA_F = HA_HEADS * HA_DK
HA_W = HA_HEADS * HA_DV
HA_CHUNK = 64
RB_HEADS = 8
RB_N = 64
RB_W = RB_HEADS * RB_N
RB_DECAY_LORA = 64
RB_AAA_LORA = 64
RB_GATE_LORA = 128
RB_PROJ_W = 3 * RB_W + RB_DECAY_LORA + RB_AAA_LORA + RB_GATE_LORA
RB_DECAY_SCALE = math.exp(-0.5)
RB_GN_EPS = 64e-5
PROJ_SPLITS = (HA_F, HA_F, HA_W, HA_W, RB_PROJ_W, D_MODEL, D_MODEL)
PROJ_W = sum(PROJ_SPLITS)
RB_SPLITS = (RB_W, RB_W, RB_W, RB_DECAY_LORA, RB_AAA_LORA, RB_GATE_LORA)
FFN_HIDDEN = -(-8 * D_MODEL // (3 * 256)) * 256
RMS_EPS = 1e-6

kernel_name = "hgrn2_rwkv7_gated_hybrid_step"


def _offsets(widths):
    return [int(o) for o in np.cumsum(widths)[:-1]]


def rmsnorm(x, g):
    xf = x.astype(jnp.float32)
    y = xf * lax.rsqrt(jnp.mean(xf * xf, axis=-1, keepdims=True) + RMS_EPS)
    return (y * g.astype(jnp.float32)).astype(x.dtype)


def gla_chunked(q, k, v, logf, S0, chunk):
    B, L, H, K = q.shape
    V = v.shape[-1]
    n = L // chunk

    def blocks(t):
        return t.reshape(B, n, chunk, H, t.shape[-1]).swapaxes(0, 1)

    causal = jnp.tril(jnp.ones((chunk, chunk), dtype=bool))

    def step(S, inp):
        qc, kc, vc, gc = inp
        b = jnp.cumsum(gc, axis=1)
        inter = jnp.einsum('bthk,bhkv->bthv', qc * jnp.exp(b), S)
        diff = b[:, :, None] - b[:, None, :]
        decay = jnp.exp(jnp.where(causal[None, :, :, None, None], diff, -jnp.inf))
        scores = jnp.einsum('bthk,btshk,bshk->btsh', qc, decay, kc)
        intra = jnp.einsum('btsh,bshv->bthv', scores, vc)
        b_last = b[:, -1]
        S_new = S * jnp.exp(b_last)[..., None] + jnp.einsum(
            'bshk,bshv->bhkv', kc * jnp.exp(b_last[:, None] - b), vc)
        return S_new, inter + intra

    S, o = lax.scan(step, S0, (blocks(q), blocks(k), blocks(v), blocks(logf)))
    return o.swapaxes(0, 1).reshape(B, L, H, V), S


def rwkv7_scan(r, w, k, v, kk, a, S0):
    def step(S, inp):
        rt, wt, kt, vt, kkt, at = inp
        sa = jnp.einsum('bhij,bhj->bhi', S, -kkt)
        S = S * wt[:, :, None, :] + sa[..., None] * (kkt * at)[:, :, None, :] + vt[..., None] * kt[:, :, None, :]
        y = jnp.einsum('bhij,bhj->bhi', S, rt)
        return S, y

    xs = tuple(t.swapaxes(0, 1) for t in (r, w, k, v, kk, a))
    S, y = lax.scan(step, S0, xs)
    return y.swapaxes(0, 1), S


def hgrn2_branch(pq, pf, pi, pg, S0, lb, norm_g):
    B, L, _ = pq.shape
    f32 = jnp.float32
    q = jax.nn.silu(pq.astype(f32)).reshape(B, L, HA_HEADS, HA_DK)
    z = pf.astype(f32)
    lb = lb.astype(f32)
    logf = jnp.logaddexp(jnp.log(lb), jnp.log1p(-lb) + jax.nn.log_sigmoid(z))
    k = (1.0 - jnp.exp(logf)).reshape(B, L, HA_HEADS, HA_DK)
    logf = logf.reshape(B, L, HA_HEADS, HA_DK)
    v = pi.astype(f32).reshape(B, L, HA_HEADS, HA_DV)
    o, S = gla_chunked(q, k, v, logf, S0.astype(f32), math.gcd(L, HA_CHUNK))
    o = o * lax.rsqrt(jnp.mean(o * o, axis=-1, keepdims=True) + RMS_EPS)
    o = o.reshape(B, L, HA_W) * norm_g.astype(f32) * jax.nn.silu(pg.astype(f32))
    return o.astype(pq.dtype), S


def rwkv7_branch(P, S0, shift0, p):
    B, L, _ = P.shape
    f32 = jnp.float32
    P_prev = jnp.concatenate([shift0.astype(P.dtype)[:, None], P[:, :-1]], axis=1)
    Pm = P + (P_prev - P) * p['rwkv_mu']
    r, k, v, dw, da, dg = jnp.split(Pm, _offsets(RB_SPLITS), axis=-1)
    w = jnp.exp(-RB_DECAY_SCALE * jax.nn.sigmoid(
        (p['rwkv_w0'] + jnp.tanh(dw) @ p['rwkv_w2']).astype(f32)))
    a = jax.nn.sigmoid((p['rwkv_a0'] + da @ p['rwkv_a2']).astype(f32))
    g = jax.nn.sigmoid(dg) @ p['rwkv_g2']
    r = r.astype(f32)
    k = k.astype(f32)
    v = v.astype(f32)
    hs = (B, L, RB_HEADS, RB_N)
    kk = (k * p['rwkv_k_k'].astype(f32)).reshape(hs)
    kk = kk / jnp.maximum(jnp.sqrt(jnp.sum(kk * kk, axis=-1, keepdims=True)), 1e-12)
    k = k * (1.0 + (a - 1.0) * p['rwkv_k_a'].astype(f32))
    r4, k4, v4, w4, a4 = r.reshape(hs), k.reshape(hs), v.reshape(hs), w.reshape(hs), a.reshape(hs)
    y, S = rwkv7_scan(r4, w4, k4, v4, kk, a4, S0.astype(f32))
    mu = jnp.mean(y, axis=-1, keepdims=True)
    var = jnp.mean(jnp.square(y - mu), axis=-1, keepdims=True)
    yn = ((y - mu) * lax.rsqrt(var + RB_GN_EPS)).reshape(B, L, RB_W)
    yn = yn * p['rwkv_gn_w'].astype(f32) + p['rwkv_gn_b'].astype(f32)
    bonus = jnp.sum(r4 * k4 * p['rwkv_r_k'].astype(f32), axis=-1, keepdims=True) * v4
    out = (yn + bonus.reshape(B, L, RB_W)).astype(P.dtype) * g
    return out, S, P[:, -1]


def mixer(xm, S_h, S_r, shift, p, lb):
    P = xm @ p['w_in']
    hq, hf, hi, hg, prw, gate_a, gate_b = jnp.split(P, _offsets(PROJ_SPLITS), axis=-1)
    oa, S_h = hgrn2_branch(hq, hf, hi, hg, S_h, lb, p['hgrn_norm'])
    ob, S_r, shift = rwkv7_branch(prw, S_r, shift, p)
    merged = jax.nn.sigmoid(gate_a) * (oa @ p['w_up_a']) + jax.nn.sigmoid(gate_b) * (ob @ p['w_up_b'])
    return merged @ p['w_out'], S_h, S_r, shift


def block(x, c, S_h, S_r, shift, p, lb):
    mod = (jax.nn.silu(c) @ p['w_ada'] + p['b_ada'])[:, None, :]
    sh1, sc1, g1, sh2, sc2, g2 = jnp.split(mod, 6, axis=-1)
    xm = rmsnorm(x, p['norm_mix']) * (1.0 + sc1) + sh1
    mix, S_h, S_r, shift = mixer(xm, S_h, S_r, shift, p, lb)
    x = x + g1 * mix
    xf = rmsnorm(x, p['norm_ffn']) * (1.0 + sc2) + sh2
    gate, up = jnp.split(xf @ p['w_ffn_in'], 2, axis=-1)
    x = x + g2 * ((jax.nn.silu(gate) * up) @ p['w_ffn_out'])
    return x, S_h, S_r, shift


def setup_inputs(seed: int = 0) -> dict:
    key = jax.random.key(seed)
    ks = iter(jax.random.split(key, 48))

    def nrm(shape, scale):
        return jax.random.normal(next(ks), shape, jnp.float32) * scale

    D = D_MODEL
    inp = {
        'x_prompt': nrm((BATCH, SEQ, D), 1.0),
        'x_sample': nrm((DEC_BATCH, DEC_SEQ, D), 1.0),
        'c_prompt': nrm((BATCH, D), 1.0),
        'c_sample': nrm((DEC_BATCH, D), 1.0),
        'state_hgrn': nrm((DEPTH, DEC_BATCH, HA_HEADS, HA_DK, HA_DV), 0.5),
        'state_rwkv': nrm((DEPTH, DEC_BATCH, RB_HEADS, RB_N, RB_N), 0.3),
        'state_shift': nrm((DEPTH, DEC_BATCH, RB_PROJ_W), 1.0),
        'w_ada': nrm((DEPTH, D, 6 * D), D ** -0.5),
        'b_ada': nrm((DEPTH, 6 * D), 0.02),
        'norm_mix': 1.0 + nrm((DEPTH, D), 0.02),
        'norm_ffn': 1.0 + nrm((DEPTH, D), 0.02),
        'w_in': nrm((DEPTH, D, PROJ_W), D ** -0.5),
        'hgrn_lower_bounds': nrm((DEPTH + 1, HA_F), 1.0),
        'hgrn_norm': 1.0 + nrm((DEPTH, HA_W), 0.02),
        'rwkv_mu': jax.random.uniform(next(ks), (DEPTH, RB_PROJ_W), jnp.float32),
        'rwkv_w0': nrm((DEPTH, RB_W), 0.5),
        'rwkv_w2': nrm((DEPTH, RB_DECAY_LORA, RB_W), RB_DECAY_LORA ** -0.5),
        'rwkv_a0': nrm((DEPTH, RB_W), 0.5),
        'rwkv_a2': nrm((DEPTH, RB_AAA_LORA, RB_W), RB_AAA_LORA ** -0.5),
        'rwkv_g2': nrm((DEPTH, RB_GATE_LORA, RB_W), RB_GATE_LORA ** -0.5),
        'rwkv_k_k': 0.85 + nrm((DEPTH, RB_W), 0.05),
        'rwkv_k_a': 1.0 + nrm((DEPTH, RB_W), 0.05),
        'rwkv_r_k': nrm((DEPTH, RB_HEADS, RB_N), 0.1),
        'rwkv_gn_w': 1.0 + nrm((DEPTH, RB_W), 0.02),
        'rwkv_gn_b': nrm((DEPTH, RB_W), 0.02),
        'w_up_a': nrm((DEPTH, HA_W, D), HA_W ** -0.5),
        'w_up_b': nrm((DEPTH, RB_W, D), RB_W ** -0.5),
        'w_out': nrm((DEPTH, D, D), D ** -0.5),
        'w_ffn_in': nrm((DEPTH, D, 2 * FFN_HIDDEN), D ** -0.5),
        'w_ffn_out': nrm((DEPTH, FFN_HIDDEN, D), FFN_HIDDEN ** -0.5),
        'norm_final': 1.0 + nrm((D,), 0.02),
    }
    return inp


def reference(x_prompt, x_sample, c_prompt, c_sample, state_hgrn, state_rwkv, state_shift,
              w_ada, b_ada, norm_mix, norm_ffn, w_in, hgrn_lower_bounds, hgrn_norm,
              rwkv_mu, rwkv_w0, rwkv_w2, rwkv_a0, rwkv_a2, rwkv_g2, rwkv_k_k, rwkv_k_a, rwkv_r_k,
              rwkv_gn_w, rwkv_gn_b, w_up_a, w_up_b, w_out, w_ffn_in, w_ffn_out, norm_final):
    f32 = jnp.float32
    lbs = jnp.cumsum(jax.nn.softmax(hgrn_lower_bounds.astype(f32), axis=0), axis=0)

    hp = jnp.zeros((BATCH, HA_HEADS, HA_DK, HA_DV), f32)
    rp = jnp.zeros((BATCH, RB_HEADS, RB_N, RB_N), f32)
    sp = jnp.zeros((BATCH, RB_PROJ_W), x_prompt.dtype)

    yp, ys = x_prompt, x_sample
    hp_out, rp_out, sp_out, hs_out, rs_out, ss_out = [], [], [], [], [], []
    for l in range(DEPTH):
        p = {
            'w_ada': w_ada[l], 'b_ada': b_ada[l], 'norm_mix': norm_mix[l], 'norm_ffn': norm_ffn[l],
            'w_in': w_in[l], 'hgrn_norm': hgrn_norm[l], 'rwkv_mu': rwkv_mu[l],
            'rwkv_w0': rwkv_w0[l], 'rwkv_w2': rwkv_w2[l], 'rwkv_a0': rwkv_a0[l], 'rwkv_a2': rwkv_a2[l],
            'rwkv_g2': rwkv_g2[l], 'rwkv_k_k': rwkv_k_k[l], 'rwkv_k_a': rwkv_k_a[l], 'rwkv_r_k': rwkv_r_k[l],
            'rwkv_gn_w': rwkv_gn_w[l], 'rwkv_gn_b': rwkv_gn_b[l], 'w_up_a': w_up_a[l], 'w_up_b': w_up_b[l],
            'w_out': w_out[l], 'w_ffn_in': w_ffn_in[l], 'w_ffn_out': w_ffn_out[l],
        }
        lb = lbs[l]
        yp, h1, r1, s1 = block(yp, c_prompt, hp, rp, sp, p, lb)
        ys, h2, r2, s2 = block(ys, c_sample, state_hgrn[l], state_rwkv[l], state_shift[l], p, lb)
        hp_out.append(h1); rp_out.append(r1); sp_out.append(s1)
        hs_out.append(h2); rs_out.append(r2); ss_out.append(s2)

    y_prompt = rmsnorm(yp, norm_final)
    y_sample = rmsnorm(ys, norm_final)
    new_hgrn_prompt = jnp.stack(hp_out)
    new_rwkv_prompt = jnp.stack(rp_out)
    new_shift_prompt = jnp.stack(sp_out)
    new_hgrn_sample = jnp.stack(hs_out)
    new_rwkv_sample = jnp.stack(rs_out)
    new_shift_sample = jnp.stack(ss_out)
    return (y_prompt, y_sample, new_hgrn_prompt, new_rwkv_prompt, new_shift_prompt,
            new_hgrn_sample, new_rwkv_sample, new_shift_sample)
```

```python
import functools
import math

import jax
import jax.numpy as jnp
from jax import lax
from jax.experimental import pallas as pl
from jax.experimental.pallas import tpu as pltpu

F32 = jnp.float32
BF16 = jnp.bfloat16
HIGHEST = lax.Precision.HIGHEST

D_MODEL = 1024
HA_HEADS = 4
HA_DK = 128
HA_W = HA_HEADS * HA_DK
RB_HEADS = 8
RB_N = 64
RB_W = RB_HEADS * RB_N
RB_LORA_WA = 128
RB_LORA_G = 128
RB_PROJ_W = 3 * RB_W + RB_LORA_WA + RB_LORA_G
RB_DECAY_SCALE = math.exp(-0.5)
RB_GN_EPS = 64e-5
HG_W = 4 * HA_W
GATE_W = 2 * D_MODEL
PROJ_W = HG_W + RB_PROJ_W + GATE_W
FFN_HIDDEN = 2816
RMS_EPS = 1e-6

V7X_LANES = 128
VMEM_LIMIT_BYTES = 56 * 1024 * 1024
RWKV_GROUP = 64


def _dot(a, b):
    return jnp.dot(a.astype(BF16), b.astype(BF16), preferred_element_type=F32)


def _dot_nt(a, b):
    return lax.dot_general(a.astype(BF16), b.astype(BF16), (((1,), (1,)), ((), ())), preferred_element_type=F32)


def _dot_tn(a, b):
    return lax.dot_general(a.astype(BF16), b.astype(BF16), (((0,), (0,)), ((), ())), preferred_element_type=F32)


def _dot_exact(a, b):
    return jnp.dot(a, b, precision=HIGHEST, preferred_element_type=F32)


def _sigmoid(x):
    return 1.0 / (1.0 + jnp.exp(-x))


def _segsum(x, seg):
    hi = x.astype(BF16)
    lo = (x - hi.astype(F32)).astype(BF16)
    return jnp.dot(hi, seg, preferred_element_type=F32) + jnp.dot(lo, seg, preferred_element_type=F32)


def _iota(shape, axis):
    return lax.broadcasted_iota(jnp.int32, shape, axis)


def _ada_kernel(c_ref, w_ref, b_ref, o_ref):
    c = c_ref[...]
    o_ref[...] = _dot(c * _sigmoid(c), w_ref[...]) + b_ref[...]


def _ada(c_all, w_ada, b_ada):
    n, d = c_all.shape
    wn = w_ada.shape[1]
    tn = 1024
    return pl.pallas_call(
        _ada_kernel,
        out_shape=jax.ShapeDtypeStruct((n, wn), F32),
        grid=(wn // tn,),
        in_specs=[pl.BlockSpec((n, d), lambda j: (0, 0)),
                  pl.BlockSpec((d, tn), lambda j: (0, j)),
                  pl.BlockSpec((1, tn), lambda j: (0, j))],
        out_specs=pl.BlockSpec((n, tn), lambda j: (0, j)),
        compiler_params=pltpu.CompilerParams(dimension_semantics=("parallel",)),
        name="ada",
    )(c_all, w_ada, b_ada.reshape(1, wn))


def _inproj_kernel(x_ref, mod_ref, g_ref, w_ref, oh_ref, or_ref, og_ref):
    bb, tl, d = x_ref.shape
    x = x_ref[...]
    xn = x * lax.rsqrt(jnp.mean(x * x, axis=-1, keepdims=True) + RMS_EPS) * g_ref[...]
    xm = xn * (1.0 + mod_ref[:, 1:2, :]) + mod_ref[:, 0:1, :]
    xb = xm.reshape(bb * tl, d).astype(BF16)
    lo = 0
    for o_ref in (oh_ref, or_ref, og_ref):
        w = o_ref.shape[-1]
        o_ref[...] = jnp.dot(xb, w_ref[:, lo:lo + w], preferred_element_type=F32).reshape(bb, tl, w)
        lo += w


def _inproj(x, mod, norm_g, w_in_bf16, bb, tl):
    B, L, d = x.shape
    const = lambda b, l: (0, 0)
    tok = lambda b, l: (b, l, 0)
    return pl.pallas_call(
        _inproj_kernel,
        out_shape=(jax.ShapeDtypeStruct((B, L, HG_W), F32),
                   jax.ShapeDtypeStruct((B, L, RB_PROJ_W), F32),
                   jax.ShapeDtypeStruct((B, L, GATE_W), F32)),
        grid=(B // bb, L // tl),
        in_specs=[pl.BlockSpec((bb, tl, d), tok),
                  pl.BlockSpec((bb, 6, d), lambda b, l: (b, 0, 0)),
                  pl.BlockSpec((1, d), const),
                  pl.BlockSpec((d, PROJ_W), const, pipeline_mode=pl.Buffered(1))],
        out_specs=(pl.BlockSpec((bb, tl, HG_W), tok),
                   pl.BlockSpec((bb, tl, RB_PROJ_W), tok),
                   pl.BlockSpec((bb, tl, GATE_W), tok)),
        compiler_params=pltpu.CompilerParams(dimension_semantics=("parallel", "parallel"),
                                             vmem_limit_bytes=VMEM_LIMIT_BYTES),
        name="inproj",
    )(x, mod, norm_g.reshape(1, d), w_in_bf16)


def _hgrn_kernel(ph_ref, s0_ref, lbp_ref, ng_ref, o_ref, s_ref, *, layer, bb, nch, C, c):
    @pl.when(pl.program_id(1) == 0)
    def _init():
        s_ref[...] = s0_ref[...]

    hb = lbp_ref[...]
    e = jnp.exp(hb - jnp.max(hb, axis=0, keepdims=True))
    lb = jnp.sum(e[0:layer + 1], axis=0, keepdims=True) / jnp.sum(e, axis=0, keepdims=True)
    ng = ng_ref[...]
    cum_m = (_iota((C, C), 0) >= _iota((C, C), 1)).astype(F32)
    ones_cv = jnp.ones((C, HA_DK), F32)
    nsub = C // c
    ngrp = c // 8
    grp_rows = _iota((8, HA_DK), 0)

    def one_chunk(bi, ci):
        r0 = pl.multiple_of(ci * C, C)
        ph = ph_ref[bi, pl.ds(r0, C), :]
        hq = ph[:, 0:HA_W]
        hf = ph[:, HA_W:2 * HA_W]
        hi = ph[:, 2 * HA_W:3 * HA_W]
        hg = ph[:, 3 * HA_W:4 * HA_W]
        q = hq * _sigmoid(hq)
        sg = _sigmoid(hf)
        logf = jnp.log(lb + (1.0 - lb) * sg)
        k = (1.0 - lb) * (1.0 - sg)
        b = _dot_exact(cum_m, logf)
        outs = []
        for h in range(HA_HEADS):
            sl = slice(h * HA_DK, (h + 1) * HA_DK)
            qh, kh, vh, bh = q[:, sl], k[:, sl], hi[:, sl], b[:, sl]
            S = s_ref[bi, h]
            o = _dot(qh * jnp.exp(bh), S)
            parts = []
            for i in range(nsub):
                lo = i * c
                qi, ki, vi, bi_ = qh[lo:lo + c], kh[lo:lo + c], vh[lo:lo + c], bh[lo:lo + c]
                acc = [jnp.zeros((8, HA_DK), F32) for _ in range(ngrp)]
                if i > 0:
                    bref = bh[lo - 1:lo, :]
                    qt = qi * jnp.exp(bi_ - bref)
                    kt = kh[:lo] * jnp.exp(bref - bh[:lo])
                    off = _dot(_dot_nt(qt, kt), vh[:lo])
                    acc = [off[8 * gidx:8 * gidx + 8] for gidx in range(ngrp)]
                for s in range(c):
                    bs, ks, vs = bi_[s:s + 1, :], ki[s:s + 1, :], vi[s:s + 1, :]
                    for gidx in range(s // 8, ngrp):
                        rs = slice(8 * gidx, 8 * gidx + 8)
                        dec = jnp.exp(jnp.minimum(bi_[rs] - bs, 0.0))
                        if gidx == s // 8:
                            dec = jnp.where(grp_rows >= (s % 8), dec, 0.0)
                        col = jnp.sum(qi[rs] * (ks * dec), axis=-1, keepdims=True)
                        acc[gidx] = acc[gidx] + col * vs
                parts.extend(acc)
            o = o + jnp.concatenate(parts, axis=0)
            bl = bh[C - 1:C, :]
            kd = kh * jnp.exp(bl - bh)
            tot = lax.dot_general(logf[:, sl], ones_cv, (((0,), (0,)), ((), ())), precision=HIGHEST,
                                  preferred_element_type=F32)
            s_ref[bi, h] = S * jnp.exp(tot) + _dot_tn(kd, vh)
            outs.append(o * lax.rsqrt(jnp.mean(o * o, axis=-1, keepdims=True) + RMS_EPS))
        o_all = jnp.concatenate(outs, axis=1) * ng * (hg * _sigmoid(hg))
        o_ref[bi, pl.ds(r0, C), :] = o_all

    def per_batch(bi, carry):
        def per_chunk(ci, carry2):
            one_chunk(bi, ci)
            return carry2
        return lax.fori_loop(0, nch, per_chunk, carry)

    lax.fori_loop(0, bb, per_batch, 0)


def _hgrn(ph, s0, lower_bounds, norm_g, layer, bb, tl, C, c):
    B, L, _ = ph.shape
    const = lambda b, l: (0, 0)
    st = lambda b, l: (b, 0, 0, 0)
    kern = functools.partial(_hgrn_kernel, layer=layer, bb=bb, nch=tl // C, C=C, c=c)
    return pl.pallas_call(
        kern,
        out_shape=(jax.ShapeDtypeStruct((B, L, HA_W), F32),
                   jax.ShapeDtypeStruct(s0.shape, F32)),
        grid=(B // bb, L // tl),
        in_specs=[pl.BlockSpec((bb, tl, HG_W), lambda b, l: (b, l, 0)),
                  pl.BlockSpec((bb, HA_HEADS, HA_DK, HA_DK), st),
                  pl.BlockSpec(lower_bounds.shape, const),
                  pl.BlockSpec((1, HA_W), const)],
        out_specs=(pl.BlockSpec((bb, tl, HA_W), lambda b, l: (b, l, 0)),
                   pl.BlockSpec((bb, HA_HEADS, HA_DK, HA_DK), st)),
        compiler_params=pltpu.CompilerParams(dimension_semantics=("parallel", "arbitrary"),
                                             vmem_limit_bytes=VMEM_LIMIT_BYTES),
        name="hgrn",
    )(ph, s0, lower_bounds, norm_g.reshape(1, HA_W))


def _rwkv_kernel(pr_ref, prev_ref, sh0_ref, s0_ref, mu_ref, w0_ref, a0_ref, wwa_ref, g2_ref, kk_ref, ka_ref,
                 rk_ref, gnw_ref, gnb_ref, o_ref, s_ref,
                 r_s, lw_s, k_s, v_s, kk_s, b_s, g_s, y_s, s2_s, *, bb, tl, C):
    l = pl.program_id(1)
    R = bb * tl
    G = RWKV_GROUP
    nb = G // C
    npair = RB_HEADS // 2

    @pl.when(l == 0)
    def _init():
        z = jnp.zeros((bb, RB_N, RB_N), F32)
        for p in range(npair):
            top = jnp.concatenate([s0_ref[:, 2 * p], z], axis=2)
            bot = jnp.concatenate([z, s0_ref[:, 2 * p + 1]], axis=2)
            s2_s[:, p] = jnp.concatenate([top, bot], axis=1)

    p_ = pr_ref[...].reshape(R, RB_PROJ_W)
    first = jnp.where(l == 0, sh0_ref[...], prev_ref[:, 7:8, :])
    rolled = pltpu.roll(p_, 1, 0).reshape(bb, tl, RB_PROJ_W)
    p_prev = jnp.where(_iota((bb, tl, RB_PROJ_W), 1) == 0, first, rolled).reshape(R, RB_PROJ_W)
    pm = p_ + (p_prev - p_) * mu_ref[...]
    r = pm[:, 0:RB_W]
    k = pm[:, RB_W:2 * RB_W]
    v = pm[:, 2 * RB_W:3 * RB_W]
    dwa = pm[:, 3 * RB_W:3 * RB_W + RB_LORA_WA]
    dg = pm[:, 3 * RB_W + RB_LORA_WA:]
    dwa = jnp.where(_iota((R, RB_LORA_WA), 1) < RB_LORA_WA // 2, jnp.tanh(dwa), dwa)
    wa = _dot(dwa, wwa_ref[...])
    lw = -RB_DECAY_SCALE * _sigmoid(w0_ref[...] + wa[:, 0:RB_W])
    a = _sigmoid(a0_ref[...] + wa[:, RB_W:])
    seg = ((_iota((RB_W, RB_W), 0) // RB_N) == (_iota((RB_W, RB_W), 1) // RB_N)).astype(BF16)
    kk = k * kk_ref[...]
    kk = kk / jnp.maximum(jnp.sqrt(_segsum(kk * kk, seg)), 1e-12)
    r_s[...] = r
    lw_s[...] = lw
    k_s[...] = k * (1.0 + (a - 1.0) * ka_ref[...])
    v_s[...] = v
    kk_s[...] = kk
    b_s[...] = kk * a
    g_s[...] = _dot(_sigmoid(dg), g2_ref[...])

    rg, cg = _iota((G, G), 0), _iota((G, G), 1)
    same_chunk = (rg // C) == (cg // C)
    cum_m = (same_chunk & (rg >= cg)).astype(F32)
    tot_m = same_chunk.astype(F32)
    r2, c2 = _iota((2 * G, 2 * G), 0), _iota((2 * G, 2 * G), 1)
    same_blk = (r2 // C) == (c2 // C)
    tri_s = same_blk & ((r2 % C) > (c2 % C))
    tri_i = same_blk & ((r2 % C) >= (c2 % C))
    head0 = _iota((G, 2 * RB_N), 1) < RB_N
    nsq = int(math.log2(C)) - 1

    def bidx(gi, j):
        if tl >= G:
            return lax.shift_right_logical(gi, int(math.log2(tl // G)))
        return gi * nb + j

    def stack2(x):
        return jnp.concatenate([jnp.where(head0, x, 0.0), jnp.where(head0, 0.0, x)], axis=0)

    def group(gi, carry):
        row0 = pl.multiple_of(gi * G, G)
        rows = pl.ds(row0, G)
        lwc = lw_s[rows, :]
        g = _dot_exact(cum_m, lwc)
        gl = _dot_exact(tot_m, lwc)
        eg, eng, egl = jnp.exp(g), jnp.exp(-g), jnp.exp(gl)
        kkc, bc, kc, rc, vc = kk_s[rows, :], b_s[rows, :], k_s[rows, :], r_s[rows, :], v_s[rows, :]
        At = -kkc * (eg * jnp.exp(-lwc))
        Bt = bc * eng
        Kt = kc * eng
        Rt = rc * eg
        dec = egl * eng
        Ke = kc * dec
        Be = bc * dec
        for p in range(npair):
            sl = slice(p * 2 * RB_N, (p + 1) * 2 * RB_N)
            Ap, Rp = At[:, sl], Rt[:, sl]
            A2, R2, B2, K2 = stack2(Ap), stack2(Rp), stack2(Bt[:, sl]), stack2(Kt[:, sl])
            V2, Ke2, Be2 = stack2(vc[:, sl]), stack2(Ke[:, sl]), stack2(Be[:, sl])
            M = _dot_nt(jnp.concatenate([A2, R2], axis=0), jnp.concatenate([B2, K2], axis=0))
            Lab = jnp.where(tri_s, M[:2 * G, :2 * G], 0.0)
            Lak = jnp.where(tri_s, M[:2 * G, 2 * G:], 0.0)
            Lrb = jnp.where(tri_i, M[2 * G:, :2 * G], 0.0)
            Lrk = jnp.where(tri_i, M[2 * G:, 2 * G:], 0.0)
            x0a, x0r, S2s = [], [], []
            for j in range(nb):
                js = slice(j * C, (j + 1) * C)
                S2 = s2_s[bidx(gi, j), p]
                x0 = _dot_nt(jnp.concatenate([Ap[js], Rp[js]], axis=0), S2)
                x0a.append(x0[:C])
                x0r.append(x0[C:])
                S2s.append(S2)
            X = stack2(jnp.concatenate(x0a, axis=0)) + _dot(Lak, V2)
            P = Lab
            SA = X + _dot(P, X)
            for _ in range(nsq):
                P = _dot(P, P)
                SA = SA + _dot(P, SA)
            Y2 = stack2(jnp.concatenate(x0r, axis=0)) + _dot(Lrk, V2) + _dot(Lrb, SA)
            y_s[rows, sl] = Y2[:G] + Y2[G:]
            for j in range(nb):
                j0, j1 = slice(j * C, (j + 1) * C), slice(G + j * C, G + (j + 1) * C)
                lhs = jnp.concatenate([V2[j0], V2[j1], SA[j0], SA[j1]], axis=0)
                rhs = jnp.concatenate([Ke2[j0], Ke2[j1], Be2[j0], Be2[j1]], axis=0)
                s2_s[bidx(gi, j), p] = S2s[j] * egl[j * C:j * C + 1, sl] + _dot_tn(lhs, rhs)
        return carry

    lax.fori_loop(0, R // G, group, 0)

    y = y_s[...]
    mean = _segsum(y, seg) * (1.0 / RB_N)
    yc = y - mean
    var = _segsum(yc * yc, seg) * (1.0 / RB_N)
    yn = yc * lax.rsqrt(var + RB_GN_EPS) * gnw_ref[...] + gnb_ref[...]
    bonus = _segsum(r_s[...] * k_s[...] * rk_ref[...], seg) * v_s[...]
    o_ref[...] = ((yn + bonus) * g_s[...]).reshape(bb, tl, RB_W)

    @pl.when(l == pl.num_programs(1) - 1)
    def _fin():
        for p in range(npair):
            S2 = s2_s[:, p]
            s_ref[:, 2 * p] = S2[:, :RB_N, :RB_N]
            s_ref[:, 2 * p + 1] = S2[:, RB_N:, RB_N:]


def _rwkv(pr, shift0, s0, mu, w0, a0, wwa, g2, k_k, k_a, r_k, gn_w, gn_b, bb, tl, C):
    B, L, _ = pr.shape
    const = lambda b, l: (0, 0)
    tok = lambda b, l: (b, l, 0)
    st = lambda b, l: (b, 0, 0, 0)
    R = bb * tl
    vec = lambda x: x.reshape(1, -1)
    kern = functools.partial(_rwkv_kernel, bb=bb, tl=tl, C=C)
    row_scratch = [pltpu.VMEM((R, RB_W), F32) for _ in range(8)]
    return pl.pallas_call(
        kern,
        out_shape=(jax.ShapeDtypeStruct((B, L, RB_W), F32),
                   jax.ShapeDtypeStruct(s0.shape, F32)),
        grid=(B // bb, L // tl),
        in_specs=[pl.BlockSpec((bb, tl, RB_PROJ_W), tok),
                  pl.BlockSpec((bb, 8, RB_PROJ_W), lambda b, l: (b, jnp.maximum(l * (tl // 8) - 1, 0), 0)),
                  pl.BlockSpec((bb, 1, RB_PROJ_W), lambda b, l: (b, 0, 0)),
                  pl.BlockSpec((bb, RB_HEADS, RB_N, RB_N), st),
                  pl.BlockSpec((1, RB_PROJ_W), const),
                  pl.BlockSpec((1, RB_W), const),
                  pl.BlockSpec((1, RB_W), const),
                  pl.BlockSpec((RB_LORA_WA, 2 * RB_W), const),
                  pl.BlockSpec((RB_LORA_G, RB_W), const),
                  pl.BlockSpec((1, RB_W), const),
                  pl.BlockSpec((1, RB_W), const),
                  pl.BlockSpec((1, RB_W), const),
                  pl.BlockSpec((1, RB_W), const),
                  pl.BlockSpec((1, RB_W), const)],
        out_specs=(pl.BlockSpec((bb, tl, RB_W), tok),
                   pl.BlockSpec((bb, RB_HEADS, RB_N, RB_N), st)),
        scratch_shapes=row_scratch + [pltpu.VMEM((bb, RB_HEADS // 2, 2 * RB_N, 2 * RB_N), F32)],
        compiler_params=pltpu.CompilerParams(dimension_semantics=("parallel", "arbitrary"),
                                             vmem_limit_bytes=VMEM_LIMIT_BYTES),
        name="rwkv",
    )(pr, pr, shift0.reshape(B, 1, RB_PROJ_W), s0, vec(mu), vec(w0), vec(a0), wwa, g2, vec(k_k), vec(k_a),
      vec(r_k), vec(gn_w), vec(gn_b))


def _mixffn_kernel(oa_ref, ob_ref, pg_ref, x_ref, mod_ref, nf_ref, nfin_ref, wua_ref, wub_ref, wo_ref, wfi_ref,
                   wfo_ref, y_ref):
    bb, tl, d = x_ref.shape
    R = bb * tl
    oa = oa_ref[...].reshape(R, HA_W)
    ob = ob_ref[...].reshape(R, RB_W)
    pg = pg_ref[...].reshape(R, GATE_W)
    merged = _sigmoid(pg[:, :d]) * _dot(oa, wua_ref[...]) + _sigmoid(pg[:, d:]) * _dot(ob, wub_ref[...])
    mix = _dot(merged, wo_ref[...]).reshape(bb, tl, d)
    x1 = x_ref[...] + mod_ref[:, 2:3, :] * mix
    xn = x1 * lax.rsqrt(jnp.mean(x1 * x1, axis=-1, keepdims=True) + RMS_EPS) * nf_ref[...]
    xf = (xn * (1.0 + mod_ref[:, 4:5, :]) + mod_ref[:, 3:4, :]).reshape(R, d)
    h = _dot(xf, wfi_ref[...])
    gate, up = h[:, :FFN_HIDDEN], h[:, FFN_HIDDEN:]
    act = gate * _sigmoid(gate) * up
    x2 = x1 + mod_ref[:, 5:6, :] * _dot(act, wfo_ref[...]).reshape(bb, tl, d)
    y_ref[...] = x2 * lax.rsqrt(jnp.mean(x2 * x2, axis=-1, keepdims=True) + RMS_EPS) * nfin_ref[...]


def _mixffn(oa, ob, pg, x, mod, norm_ffn, norm_final, wua, wub, wo, wfi, wfo, bb, tl):
    B, L, d = x.shape
    const = lambda b, l: (0, 0)
    tok = lambda b, l: (b, l, 0)
    wspec = lambda w: pl.BlockSpec(w.shape, const, pipeline_mode=pl.Buffered(1))
    return pl.pallas_call(
        _mixffn_kernel,
        out_shape=jax.ShapeDtypeStruct((B, L, d), F32),
        grid=(B // bb, L // tl),
        in_specs=[pl.BlockSpec((bb, tl, HA_W), tok),
                  pl.BlockSpec((bb, tl, RB_W), tok),
                  pl.BlockSpec((bb, tl, GATE_W), tok),
                  pl.BlockSpec((bb, tl, d), tok),
                  pl.BlockSpec((bb, 6, d), lambda b, l: (b, 0, 0)),
                  pl.BlockSpec((1, d), const),
                  pl.BlockSpec((1, d), const),
                  wspec(wua), wspec(wub), wspec(wo), wspec(wfi), wspec(wfo)],
        out_specs=pl.BlockSpec((bb, tl, d), tok),
        compiler_params=pltpu.CompilerParams(dimension_semantics=("parallel", "parallel"),
                                             vmem_limit_bytes=VMEM_LIMIT_BYTES),
        name="mixffn",
    )(oa, ob, pg, x, mod, norm_ffn.reshape(1, d), norm_final.reshape(1, d), wua, wub, wo, wfi, wfo)


def _block(x, mod, s_h, s_r, shift, p, layer, tiles):
    bb, tl, hc, hsub, rc = tiles
    ph, pr, pg = _inproj(x, mod, p['norm_mix'], p['w_in'], bb, tl)
    oa, s_h = _hgrn(ph, s_h, p['lower_bounds'], p['hgrn_norm'], layer, min(bb, 8), tl, hc, hsub)
    ob, s_r = _rwkv(pr, shift, s_r, p['mu'], p['w0'], p['a0'], p['wwa'], p['g2'], p['k_k'], p['k_a'], p['r_k'],
                    p['gn_w'], p['gn_b'], min(bb, 16), tl, rc)
    y = _mixffn(oa, ob, pg, x, mod, p['norm_ffn'], p['norm_final'], p['w_up_a'], p['w_up_b'], p['w_out'],
                p['w_ffn_in'], p['w_ffn_out'], bb, tl)
    return y, s_h, s_r, pr[:, -1]


def kernel(x_prompt, x_sample, c_prompt, c_sample, state_hgrn, state_rwkv, state_shift, w_ada, b_ada, norm_mix, norm_ffn, w_in, hgrn_lower_bounds, hgrn_norm, rwkv_mu, rwkv_w0, rwkv_w2, rwkv_a0, rwkv_a2, rwkv_g2, rwkv_k_k, rwkv_k_a, rwkv_r_k, rwkv_gn_w, rwkv_gn_b, w_up_a, w_up_b, w_out, w_ffn_in, w_ffn_out, norm_final):
    depth = w_ada.shape[0]
    assert depth == 1, "the final norm is fused into the layer kernel"
    nbp, nbs = x_prompt.shape[0], x_sample.shape[0]
    c_all = jnp.concatenate([c_prompt, c_sample], axis=0)

    hp = jnp.zeros((nbp, HA_HEADS, HA_DK, HA_DK), F32)
    rp = jnp.zeros((nbp, RB_HEADS, RB_N, RB_N), F32)
    sp = jnp.zeros((nbp, RB_PROJ_W), x_prompt.dtype)

    yp, ys = x_prompt, x_sample
    outs = [[] for _ in range(6)]
    for l in range(depth):
        half = RB_LORA_WA // 2
        zero = jnp.zeros((half, RB_W), F32)
        wwa = jnp.concatenate([jnp.concatenate([rwkv_w2[l], zero], axis=1),
                               jnp.concatenate([zero, rwkv_a2[l]], axis=1)], axis=0)
        p = {
            'norm_mix': norm_mix[l], 'norm_ffn': norm_ffn[l], 'norm_final': norm_final,
            'w_in': w_in[l].astype(BF16), 'lower_bounds': hgrn_lower_bounds, 'hgrn_norm': hgrn_norm[l],
            'mu': rwkv_mu[l], 'w0': rwkv_w0[l], 'a0': rwkv_a0[l], 'wwa': wwa.astype(BF16),
            'g2': rwkv_g2[l].astype(BF16), 'k_k': rwkv_k_k[l], 'k_a': rwkv_k_a[l], 'r_k': rwkv_r_k[l],
            'gn_w': rwkv_gn_w[l], 'gn_b': rwkv_gn_b[l], 'w_up_a': w_up_a[l].astype(BF16),
            'w_up_b': w_up_b[l].astype(BF16), 'w_out': w_out[l].astype(BF16),
            'w_ffn_in': w_ffn_in[l].astype(BF16), 'w_ffn_out': w_ffn_out[l].astype(BF16),
        }
        mod = _ada(c_all, w_ada[l], b_ada[l])
        mod_p = mod[:nbp].reshape(nbp, 6, D_MODEL)
        mod_s = mod[nbp:].reshape(nbs, 6, D_MODEL)
        yp, h1, r1, s1 = _block(yp, mod_p, hp, rp, sp, p, l, (1, 256, 64, 16, 64))
        ys, h2, r2, s2 = _block(ys, mod_s, state_hgrn[l], state_rwkv[l], state_shift[l], p, l, (32, 8, 8, 8, 8))
        for lst, val in zip(outs, (h1, r1, s1, h2, r2, s2)):
            lst.append(val)
    stacked = [jnp.stack(o) for o in outs]
    return (yp, ys, *stacked)
```

```python
import functools
import math

import jax
import jax.numpy as jnp
from jax import lax
from jax.experimental import pallas as pl
from jax.experimental.pallas import tpu as pltpu

F32 = jnp.float32
BF16 = jnp.bfloat16

D_MODEL = 1024
HA_HEADS = 4
HA_DK = 128
HA_W = HA_HEADS * HA_DK
RB_HEADS = 8
RB_N = 64
RB_W = RB_HEADS * RB_N
RB_LORA_WA = 128
RB_LORA_G = 128
RB_PROJ_W = 3 * RB_W + RB_LORA_WA + RB_LORA_G
RB_DECAY_SCALE = math.exp(-0.5)
RB_GN_EPS = 64e-5
HG_W = 4 * HA_W
GATE_W = 2 * D_MODEL
PROJ_W = HG_W + RB_PROJ_W + GATE_W
FFN_HIDDEN = 2816
RMS_EPS = 1e-6
LOG2_E = math.log2(math.e)

VMEM_LIMIT_BYTES = 56 * 1024 * 1024
UNITS_PER_ITER = 2
RWKV_GROUP = 64


def _dot(a, b):
    return jnp.dot(a.astype(BF16), b.astype(BF16), preferred_element_type=F32)


def _dot_nt(a, b):
    return lax.dot_general(a.astype(BF16), b.astype(BF16), (((1,), (1,)), ((), ())), preferred_element_type=F32)


def _dot_tn(a, b):
    return lax.dot_general(a.astype(BF16), b.astype(BF16), (((0,), (0,)), ((), ())), preferred_element_type=F32)


def _sigmoid(x):
    return 1.0 / (1.0 + jnp.exp(-x))


def _segsum(x, seg):
    return jnp.dot(x.astype(BF16), seg, preferred_element_type=F32)


def _sum3(m, x):
    hi = x.astype(BF16)
    r1 = x - hi.astype(F32)
    mid = r1.astype(BF16)
    lo = (r1 - mid.astype(F32)).astype(BF16)
    return (jnp.dot(m, hi, preferred_element_type=F32) + jnp.dot(m, mid, preferred_element_type=F32)
            + jnp.dot(m, lo, preferred_element_type=F32))


def _iota(shape, axis):
    return lax.broadcasted_iota(jnp.int32, shape, axis)


def _ada_kernel(c_ref, w_ref, b_ref, o_ref):
    c = c_ref[...]
    o_ref[...] = _dot(c * _sigmoid(c), w_ref[...]) + b_ref[...]


def _ada(c_all, w_ada, b_ada):
    n, d = c_all.shape
    wn = w_ada.shape[1]
    tn = 1024
    return pl.pallas_call(
        _ada_kernel,
        out_shape=jax.ShapeDtypeStruct((n, wn), F32),
        grid=(wn // tn,),
        in_specs=[pl.BlockSpec((n, d), lambda j: (0, 0)),
                  pl.BlockSpec((d, tn), lambda j: (0, j)),
                  pl.BlockSpec((1, tn), lambda j: (0, j))],
        out_specs=pl.BlockSpec((n, tn), lambda j: (0, j)),
        compiler_params=pltpu.CompilerParams(dimension_semantics=("parallel",)),
        name="ada",
    )(c_all, w_ada, b_ada.reshape(1, wn))


def _inproj_kernel(x_ref, mod_ref, g_ref, w_ref, oh_ref, or_ref, og_ref):
    bb, tl, d = x_ref.shape
    x = x_ref[...]
    xn = x * lax.rsqrt(jnp.mean(x * x, axis=-1, keepdims=True) + RMS_EPS) * g_ref[...]
    xm = xn * (1.0 + mod_ref[:, 1:2, :]) + mod_ref[:, 0:1, :]
    xb = xm.reshape(bb * tl, d).astype(BF16)
    lo = 0
    for o_ref in (oh_ref, or_ref, og_ref):
        w = o_ref.shape[-1]
        o_ref[...] = jnp.dot(xb, w_ref[:, lo:lo + w], preferred_element_type=F32).reshape(bb, tl, w)
        lo += w


def _inproj(x, mod, norm_g, w_in_bf16, bb, tl):
    B, L, d = x.shape
    const = lambda b, l: (0, 0)
    tok = lambda b, l: (b, l, 0)
    return pl.pallas_call(
        _inproj_kernel,
        out_shape=(jax.ShapeDtypeStruct((B, L, HG_W), F32),
                   jax.ShapeDtypeStruct((B, L, RB_PROJ_W), F32),
                   jax.ShapeDtypeStruct((B, L, GATE_W), F32)),
        grid=(B // bb, L // tl),
        in_specs=[pl.BlockSpec((bb, tl, d), tok),
                  pl.BlockSpec((bb, 6, d), lambda b, l: (b, 0, 0)),
                  pl.BlockSpec((1, d), const),
                  pl.BlockSpec((d, PROJ_W), const, pipeline_mode=pl.Buffered(1))],
        out_specs=(pl.BlockSpec((bb, tl, HG_W), tok),
                   pl.BlockSpec((bb, tl, RB_PROJ_W), tok),
                   pl.BlockSpec((bb, tl, GATE_W), tok)),
        compiler_params=pltpu.CompilerParams(dimension_semantics=("parallel", "parallel"),
                                             vmem_limit_bytes=VMEM_LIMIT_BYTES),
        name="inproj",
    )(x, mod, norm_g.reshape(1, d), w_in_bf16)


def _hgrn_kernel(ph_ref, s0_ref, lbp_ref, ng_ref, o_ref, s_ref, c2_s, *, layer, bb, nch, C, c):
    @pl.when(pl.program_id(1) == 0)
    def _init():
        s_ref[...] = s0_ref[...]

    hb = lbp_ref[...]
    e = jnp.exp(hb - jnp.max(hb, axis=0, keepdims=True))
    lb = jnp.sum(e[0:layer + 1], axis=0, keepdims=True) / jnp.sum(e, axis=0, keepdims=True)
    ng = ng_ref[...]
    cum_m = (_iota((C, C), 0) >= _iota((C, C), 1)).astype(BF16)
    nsub = C // c
    causal = _iota((C, C), 0) >= _iota((C, C), 1)

    def one_chunk(ci):
        r0 = pl.multiple_of(ci * C, C)
        tasks, units = [], []
        for u in range(bb):
            ph = ph_ref[u, pl.ds(r0, C), :]
            hq = ph[:, 0:HA_W]
            hf = ph[:, HA_W:2 * HA_W]
            hi = ph[:, 2 * HA_W:3 * HA_W]
            hg = ph[:, 3 * HA_W:4 * HA_W]
            q = hq * _sigmoid(hq)
            sg = _sigmoid(hf)
            logf = jnp.log(lb + (1.0 - lb) * sg)
            b2 = _sum3(cum_m, logf) * LOG2_E
            c2 = b2 - jnp.log2((1.0 - lb) * (1.0 - sg))
            c2_s[u] = c2
            unit = dict(u=u, gate=hg * _sigmoid(hg), tasks=[])
            for h in range(HA_HEADS):
                sl = slice(h * HA_DK, (h + 1) * HA_DK)
                t = dict(u=u, h=h, q=q[:, sl], v=hi[:, sl], b2=b2[:, sl], c2=c2[:, sl])
                tasks.append(t)
                unit['tasks'].append(t)
            units.append(unit)
        for t in tasks:
            t['S'] = s_ref[t['u'], t['h']]
        for t in tasks:
            t['o'] = _dot(t['q'] * jnp.exp2(t['b2']), t['S'])
        lane_is = [_iota((c, C), 1) == j for j in range(C)]
        for t in tasks:
            q, b2, c2 = t['q'], t['b2'], t['c2']
            sl = slice(t['h'] * HA_DK, (t['h'] + 1) * HA_DK)
            rows = []
            for i in range(nsub):
                lo = i * c
                qi, bi = q[lo:lo + c], b2[lo:lo + c]
                if i > 0:
                    bref = b2[lo - 1:lo, :]
                    kt = jnp.concatenate([jnp.exp2(bref - c2[:lo]), jnp.zeros((C - lo, HA_DK), F32)], axis=0)
                    sc = _dot_nt(qi * jnp.exp2(bi - bref), kt)
                else:
                    sc = jnp.zeros((c, C), F32)
                for s in range(c):
                    col = jnp.sum(qi * jnp.exp2(bi - c2_s[t['u'], lo + s:lo + s + 1, sl]), axis=-1, keepdims=True)
                    sc = jnp.where(lane_is[lo + s], col, sc)
                rows.append(sc)
            t['scores'] = jnp.where(causal, jnp.concatenate(rows, axis=0), 0.0)
        for t in tasks:
            t['o'] = t['o'] + _dot(t['scores'], t['v'])
        for t in tasks:
            bl = t['b2'][C - 1:C, :]
            kd = jnp.exp2(bl - t['c2'])
            dcol = jnp.transpose(jnp.broadcast_to(jnp.exp2(bl), (8, HA_DK)))[:, 0:1]
            s_ref[t['u'], t['h']] = t['S'] * dcol + _dot_tn(kd, t['v'])
        for unit in units:
            outs = [t['o'] * lax.rsqrt(jnp.mean(t['o'] * t['o'], axis=-1, keepdims=True) + RMS_EPS)
                    for t in unit['tasks']]
            o_ref[unit['u'], pl.ds(r0, C), :] = jnp.concatenate(outs, axis=1) * ng * unit['gate']

    def per_chunk(ci, carry):
        one_chunk(ci)
        return carry

    lax.fori_loop(0, nch, per_chunk, 0)


def _hgrn(ph, s0, lower_bounds, norm_g, layer, bb, tl, C, c):
    B, L, _ = ph.shape
    const = lambda b, l: (0, 0)
    st = lambda b, l: (b, 0, 0, 0)
    kern = functools.partial(_hgrn_kernel, layer=layer, bb=bb, nch=tl // C, C=C, c=c)
    return pl.pallas_call(
        kern,
        out_shape=(jax.ShapeDtypeStruct((B, L, HA_W), F32),
                   jax.ShapeDtypeStruct(s0.shape, F32)),
        grid=(B // bb, L // tl),
        in_specs=[pl.BlockSpec((bb, tl, HG_W), lambda b, l: (b, l, 0)),
                  pl.BlockSpec((bb, HA_HEADS, HA_DK, HA_DK), st),
                  pl.BlockSpec(lower_bounds.shape, const),
                  pl.BlockSpec((1, HA_W), const)],
        out_specs=(pl.BlockSpec((bb, tl, HA_W), lambda b, l: (b, l, 0)),
                   pl.BlockSpec((bb, HA_HEADS, HA_DK, HA_DK), st)),
        scratch_shapes=[pltpu.VMEM((bb, C, HA_W), F32)],
        compiler_params=pltpu.CompilerParams(dimension_semantics=("parallel", "arbitrary"),
                                             vmem_limit_bytes=VMEM_LIMIT_BYTES),
        name="hgrn",
    )(ph, s0, lower_bounds, norm_g.reshape(1, HA_W))


def _rwkv_kernel(pr_ref, prev_ref, sh0_ref, s0_ref, mu_ref, w0_ref, a0_ref, wwa_ref, g2_ref, kk_ref, ka_ref,
                 rk_ref, gnw_ref, gnb_ref, o_ref, s_ref, s2_s, *, bb, tl, C):
    l = pl.program_id(1)
    G = RWKV_GROUP
    nb = G // C
    npair = RB_HEADS // 2
    W = RB_PROJ_W

    @pl.when(l == 0)
    def _init():
        z = jnp.zeros((bb, RB_N, RB_N), F32)
        for p in range(npair):
            top = jnp.concatenate([s0_ref[:, 2 * p], z], axis=2)
            bot = jnp.concatenate([z, s0_ref[:, 2 * p + 1]], axis=2)
            s2_s[:, p] = jnp.concatenate([top, bot], axis=1)

    rg, cg = _iota((G, G), 0), _iota((G, G), 1)
    same_chunk = (rg // C) == (cg // C)
    cum_m = (same_chunk & (rg >= cg)).astype(BF16)
    if nb > 1:
        cum_m = jnp.concatenate([cum_m, same_chunk.astype(BF16)], axis=0)
    rp, cp = _iota((G, 2 * RB_N), 0), _iota((G, 2 * RB_N), 1) % RB_N
    same_c = (rp // C) == (cp // C)
    tri_s = same_c & ((rp % C) > (cp % C))
    tri_i = same_c & ((rp % C) >= (cp % C))
    head0 = _iota((G, 2 * RB_N), 1) < RB_N
    same_head = (_iota((2 * RB_N, 2 * RB_N), 0) // RB_N) == (_iota((2 * RB_N, 2 * RB_N), 1) // RB_N)
    seg = ((_iota((RB_W, RB_W), 0) // RB_N) == (_iota((RB_W, RB_W), 1) // RB_N)).astype(BF16)
    is_tanh = _iota((G, RB_LORA_WA), 1) < RB_LORA_WA // 2
    nsq = int(math.log2(C)) - 1

    def stack2(x):
        return jnp.concatenate([jnp.where(head0, x, 0.0), jnp.where(head0, 0.0, x)], axis=0)

    if tl >= G:
        n_iter, n_unit = tl // G, bb
    else:
        n_unit = min(UNITS_PER_ITER, bb * tl // G)
        n_iter = bb * tl // (G * n_unit)
    sls = [slice(p * 2 * RB_N, (p + 1) * 2 * RB_N) for p in range(npair)]

    def load_rows(it, u):
        if tl >= G:
            r0 = pl.multiple_of(it * G, G)
            x = pr_ref[u, pl.ds(r0, G), :]
            before = pr_ref[u, pl.ds(jnp.maximum(r0 - 1, 0), 1), :]
            head = jnp.where(l == 0, sh0_ref[u], prev_ref[u, 7:8, :])
            first = jnp.where(it == 0, head, before)
            prev = jnp.where(_iota((G, W), 0) == 0, first, pltpu.roll(x, 1, 0))
            return x, prev, [u], (u, pl.ds(r0, G))
        b0 = pl.multiple_of((it * n_unit + u) * nb, nb)
        x = pr_ref[pl.ds(b0, nb)]
        rolled = pltpu.roll(x.reshape(G, W), 1, 0).reshape(nb, tl, W)
        prev = jnp.where(_iota((nb, tl, W), 1) == 0, sh0_ref[pl.ds(b0, nb)], rolled)
        return x.reshape(G, W), prev.reshape(G, W), [b0 + j for j in range(nb)], (pl.ds(b0, nb),)

    def step(it, carry):
        units = []
        for u in range(n_unit):
            x, xprev, bs, out_idx = load_rows(it, u)
            pm = x + (xprev - x) * mu_ref[...]
            dwa = pm[:, 3 * RB_W:3 * RB_W + RB_LORA_WA]
            units.append(dict(bs=bs, out_idx=out_idx, r=pm[:, 0:RB_W], k=pm[:, RB_W:2 * RB_W],
                              v=pm[:, 2 * RB_W:3 * RB_W], dwa=jnp.where(is_tanh, jnp.tanh(dwa), dwa),
                              sdg=_sigmoid(pm[:, 3 * RB_W + RB_LORA_WA:])))
        wa = _dot(jnp.concatenate([q['dwa'] for q in units], axis=0), wwa_ref[...])
        gate = _dot(jnp.concatenate([q['sdg'] for q in units], axis=0), g2_ref[...])
        for i, q in enumerate(units):
            rs = slice(i * G, (i + 1) * G)
            q['lw'] = -RB_DECAY_SCALE * _sigmoid(w0_ref[...] + wa[rs, 0:RB_W])
            q['a'] = _sigmoid(a0_ref[...] + wa[rs, RB_W:])
            q['gate'] = gate[rs]
            q['kk'] = q['k'] * kk_ref[...]
            q['kp'] = q['k'] * (1.0 + (q['a'] - 1.0) * ka_ref[...])
        sums = _segsum(jnp.concatenate([q['kk'] * q['kk'] for q in units]
                                       + [q['r'] * q['kp'] * rk_ref[...] for q in units], axis=0), seg)
        tasks = []
        for i, q in enumerate(units):
            kk = q['kk'] * lax.rsqrt(jnp.maximum(sums[i * G:(i + 1) * G], 1e-24))
            q['bonus'] = sums[(n_unit + i) * G:(n_unit + i + 1) * G] * q['v']
            kb = kk * q['a']
            lw, kp = q['lw'], q['kp']
            gs = _sum3(cum_m, lw)
            g = gs[:G]
            gl = gs[G:] if nb > 1 else g[G - 1:G, :]
            eg, eng, egl = jnp.exp(g), jnp.exp(-g), jnp.exp(gl)
            At = -kk * (eg * jnp.exp(-lw))
            Bt = kb * eng
            Kt = kp * eng
            Rt = q['r'] * eg
            dec = egl * eng
            Ke = kp * dec
            Be = kb * dec
            q['tasks'] = []
            for p in range(npair):
                sl = sls[p]
                t = dict(p=p, bs=q['bs'], AR=jnp.concatenate([At[:, sl], Rt[:, sl]], axis=0), B=Bt[:, sl],
                         K=Kt[:, sl], V=q['v'][:, sl], Ke=Ke[:, sl], Be=Be[:, sl], egl=egl[:, sl])
                tasks.append(t)
                q['tasks'].append(t)
        for t in tasks:
            t['S2'] = [s2_s[b, t['p']] for b in t['bs']]
        for t in tasks:
            t['M'] = _dot_nt(t['AR'], jnp.concatenate([stack2(t['B']), stack2(t['K'])], axis=0))
        for t in tasks:
            if nb == 1:
                x0 = _dot_nt(t['AR'], t['S2'][0])
                t['x0a'], t['x0r'] = x0[:G], x0[G:]
            else:
                x0 = [_dot_nt(jnp.concatenate([t['AR'][j * C:(j + 1) * C], t['AR'][G + j * C:G + (j + 1) * C]],
                                              axis=0), t['S2'][j]) for j in range(nb)]
                t['x0a'] = jnp.concatenate([x[:C] for x in x0], axis=0)
                t['x0r'] = jnp.concatenate([x[C:] for x in x0], axis=0)
        for t in tasks:
            M = t['M']
            lak = jnp.where(tri_s, M[:G, 2 * RB_N:], 0.0)
            lrk = jnp.where(tri_i, M[G:, 2 * RB_N:], 0.0)
            lv = _dot(jnp.concatenate([lak, lrk], axis=0), stack2(t['V']))
            t['X'] = t['x0a'] + lv[:G]
            t['Yv'] = t['x0r'] + lv[G:]
            t['P'] = jnp.where(tri_s, M[:G, :2 * RB_N], 0.0)
        for t in tasks:
            t['SA'] = t['X'] + _dot(t['P'], stack2(t['X']))
        for _ in range(nsq):
            for t in tasks:
                t['P'] = _dot(t['P'], stack2(t['P']))
            for t in tasks:
                t['SA'] = t['SA'] + _dot(t['P'], stack2(t['SA']))
        for t in tasks:
            lrb = jnp.where(tri_i, t['M'][G:, :2 * RB_N], 0.0)
            t['y'] = t['Yv'] + _dot(lrb, stack2(t['SA']))
        for t in tasks:
            for j in range(nb):
                js = slice(j * C, (j + 1) * C)
                upd = _dot_tn(jnp.concatenate([t['V'][js], t['SA'][js]], axis=0),
                              jnp.concatenate([t['Ke'][js], t['Be'][js]], axis=0))
                s2_s[t['bs'][j], t['p']] = t['S2'][j] * t['egl'][j * C:j * C + 1, :] + jnp.where(same_head, upd, 0.0)
        y = jnp.concatenate([jnp.concatenate([t['y'] for t in q['tasks']], axis=1) for q in units], axis=0)
        yc = y - _segsum(y, seg) * (1.0 / RB_N)
        var = _segsum(yc * yc, seg) * (1.0 / RB_N)
        yn = yc * lax.rsqrt(var + RB_GN_EPS) * gnw_ref[...] + gnb_ref[...]
        for i, q in enumerate(units):
            out = (yn[i * G:(i + 1) * G] + q['bonus']) * q['gate']
            if tl >= G:
                o_ref[q['out_idx']] = out
            else:
                o_ref[q['out_idx']] = out.reshape(nb, tl, RB_W)
        return carry

    lax.fori_loop(0, n_iter, step, 0)

    @pl.when(l == pl.num_programs(1) - 1)
    def _fin():
        for p in range(npair):
            S2 = s2_s[:, p]
            s_ref[:, 2 * p] = S2[:, :RB_N, :RB_N]
            s_ref[:, 2 * p + 1] = S2[:, RB_N:, RB_N:]


def _rwkv(pr, shift0, s0, mu, w0, a0, wwa, g2, k_k, k_a, r_k, gn_w, gn_b, bb, tl, C):
    B, L, _ = pr.shape
    const = lambda b, l: (0, 0)
    tok = lambda b, l: (b, l, 0)
    st = lambda b, l: (b, 0, 0, 0)
    vec = lambda x: x.reshape(1, -1)
    kern = functools.partial(_rwkv_kernel, bb=bb, tl=tl, C=C)
    return pl.pallas_call(
        kern,
        out_shape=(jax.ShapeDtypeStruct((B, L, RB_W), F32),
                   jax.ShapeDtypeStruct(s0.shape, F32)),
        grid=(B // bb, L // tl),
        in_specs=[pl.BlockSpec((bb, tl, RB_PROJ_W), tok),
                  pl.BlockSpec((bb, 8, RB_PROJ_W), lambda b, l: (b, jnp.maximum(l * (tl // 8) - 1, 0), 0)),
                  pl.BlockSpec((bb, 1, RB_PROJ_W), lambda b, l: (b, 0, 0)),
                  pl.BlockSpec((bb, RB_HEADS, RB_N, RB_N), st),
                  pl.BlockSpec((1, RB_PROJ_W), const),
                  pl.BlockSpec((1, RB_W), const),
                  pl.BlockSpec((1, RB_W), const),
                  pl.BlockSpec((RB_LORA_WA, 2 * RB_W), const),
                  pl.BlockSpec((RB_LORA_G, RB_W), const),
                  pl.BlockSpec((1, RB_W), const),
                  pl.BlockSpec((1, RB_W), const),
                  pl.BlockSpec((1, RB_W), const),
                  pl.BlockSpec((1, RB_W), const),
                  pl.BlockSpec((1, RB_W), const)],
        out_specs=(pl.BlockSpec((bb, tl, RB_W), tok),
                   pl.BlockSpec((bb, RB_HEADS, RB_N, RB_N), st)),
        scratch_shapes=[pltpu.VMEM((bb, RB_HEADS // 2, 2 * RB_N, 2 * RB_N), F32)],
        compiler_params=pltpu.CompilerParams(dimension_semantics=("parallel", "arbitrary"),
                                             vmem_limit_bytes=VMEM_LIMIT_BYTES),
        name="rwkv",
    )(pr, pr, shift0.reshape(B, 1, RB_PROJ_W), s0, vec(mu), vec(w0), vec(a0), wwa, g2, vec(k_k), vec(k_a),
      vec(r_k), vec(gn_w), vec(gn_b))


def _mixffn_kernel(oa_ref, ob_ref, pg_ref, x_ref, mod_ref, nf_ref, nfin_ref, wua_ref, wub_ref, wo_ref, wfi_ref,
                   wfo_ref, y_ref):
    bb, tl, d = x_ref.shape
    R = bb * tl
    oa = oa_ref[...].reshape(R, HA_W)
    ob = ob_ref[...].reshape(R, RB_W)
    pg = pg_ref[...].reshape(R, GATE_W)
    merged = _sigmoid(pg[:, :d]) * _dot(oa, wua_ref[...]) + _sigmoid(pg[:, d:]) * _dot(ob, wub_ref[...])
    mix = _dot(merged, wo_ref[...]).reshape(bb, tl, d)
    x1 = x_ref[...] + mod_ref[:, 2:3, :] * mix
    xn = x1 * lax.rsqrt(jnp.mean(x1 * x1, axis=-1, keepdims=True) + RMS_EPS) * nf_ref[...]
    xf = (xn * (1.0 + mod_ref[:, 4:5, :]) + mod_ref[:, 3:4, :]).reshape(R, d)
    h = _dot(xf, wfi_ref[...])
    gate, up = h[:, :FFN_HIDDEN], h[:, FFN_HIDDEN:]
    act = gate * _sigmoid(gate) * up
    x2 = x1 + mod_ref[:, 5:6, :] * _dot(act, wfo_ref[...]).reshape(bb, tl, d)
    y_ref[...] = x2 * lax.rsqrt(jnp.mean(x2 * x2, axis=-1, keepdims=True) + RMS_EPS) * nfin_ref[...]


def _mixffn(oa, ob, pg, x, mod, norm_ffn, norm_final, wua, wub, wo, wfi, wfo, bb, tl):
    B, L, d = x.shape
    const = lambda b, l: (0, 0)
    tok = lambda b, l: (b, l, 0)
    wspec = lambda w: pl.BlockSpec(w.shape, const, pipeline_mode=pl.Buffered(1))
    return pl.pallas_call(
        _mixffn_kernel,
        out_shape=jax.ShapeDtypeStruct((B, L, d), F32),
        grid=(B // bb, L // tl),
        in_specs=[pl.BlockSpec((bb, tl, HA_W), tok),
                  pl.BlockSpec((bb, tl, RB_W), tok),
                  pl.BlockSpec((bb, tl, GATE_W), tok),
                  pl.BlockSpec((bb, tl, d), tok),
                  pl.BlockSpec((bb, 6, d), lambda b, l: (b, 0, 0)),
                  pl.BlockSpec((1, d), const),
                  pl.BlockSpec((1, d), const),
                  wspec(wua), wspec(wub), wspec(wo), wspec(wfi), wspec(wfo)],
        out_specs=pl.BlockSpec((bb, tl, d), tok),
        compiler_params=pltpu.CompilerParams(dimension_semantics=("parallel", "parallel"),
                                             vmem_limit_bytes=VMEM_LIMIT_BYTES),
        name="mixffn",
    )(oa, ob, pg, x, mod, norm_ffn.reshape(1, d), norm_final.reshape(1, d), wua, wub, wo, wfi, wfo)


def _block(x, mod, s_h, s_r, shift, p, layer, tiles):
    ph, pr, pg = _inproj(x, mod, p['norm_mix'], p['w_in'], *tiles['dense'])
    oa, s_h = _hgrn(ph, s_h, p['lower_bounds'], p['hgrn_norm'], layer, *tiles['hgrn'])
    ob, s_r = _rwkv(pr, shift, s_r, p['mu'], p['w0'], p['a0'], p['wwa'], p['g2'], p['k_k'], p['k_a'], p['r_k'],
                    p['gn_w'], p['gn_b'], *tiles['rwkv'])
    y = _mixffn(oa, ob, pg, x, mod, p['norm_ffn'], p['norm_final'], p['w_up_a'], p['w_up_b'], p['w_out'],
                p['w_ffn_in'], p['w_ffn_out'], *tiles['dense'])
    return y, s_h, s_r, pr[:, -1]


PROMPT_TILES = {'dense': (1, 256), 'hgrn': (4, 256, 64, 8), 'rwkv': (4, 256, 64)}
SAMPLE_TILES = {'dense': (32, 8), 'hgrn': (8, 8, 8, 8), 'rwkv': (16, 8, 8)}


def kernel(x_prompt, x_sample, c_prompt, c_sample, state_hgrn, state_rwkv, state_shift, w_ada, b_ada, norm_mix, norm_ffn, w_in, hgrn_lower_bounds, hgrn_norm, rwkv_mu, rwkv_w0, rwkv_w2, rwkv_a0, rwkv_a2, rwkv_g2, rwkv_k_k, rwkv_k_a, rwkv_r_k, rwkv_gn_w, rwkv_gn_b, w_up_a, w_up_b, w_out, w_ffn_in, w_ffn_out, norm_final):
    depth = w_ada.shape[0]
    assert depth == 1, "the final norm is fused into the layer kernel"
    nbp, nbs = x_prompt.shape[0], x_sample.shape[0]
    c_all = jnp.concatenate([c_prompt, c_sample], axis=0)

    hp = jnp.zeros((nbp, HA_HEADS, HA_DK, HA_DK), F32)
    rp = jnp.zeros((nbp, RB_HEADS, RB_N, RB_N), F32)
    sp = jnp.zeros((nbp, RB_PROJ_W), x_prompt.dtype)

    yp, ys = x_prompt, x_sample
    outs = [[] for _ in range(6)]
    for l in range(depth):
        half = RB_LORA_WA // 2
        zero = jnp.zeros((half, RB_W), F32)
        wwa = jnp.concatenate([jnp.concatenate([rwkv_w2[l], zero], axis=1),
                               jnp.concatenate([zero, rwkv_a2[l]], axis=1)], axis=0)
        p = {
            'norm_mix': norm_mix[l], 'norm_ffn': norm_ffn[l], 'norm_final': norm_final,
            'w_in': w_in[l].astype(BF16), 'lower_bounds': hgrn_lower_bounds, 'hgrn_norm': hgrn_norm[l],
            'mu': rwkv_mu[l], 'w0': rwkv_w0[l], 'a0': rwkv_a0[l], 'wwa': wwa.astype(BF16),
            'g2': rwkv_g2[l].astype(BF16), 'k_k': rwkv_k_k[l], 'k_a': rwkv_k_a[l], 'r_k': rwkv_r_k[l],
            'gn_w': rwkv_gn_w[l], 'gn_b': rwkv_gn_b[l], 'w_up_a': w_up_a[l].astype(BF16),
            'w_up_b': w_up_b[l].astype(BF16), 'w_out': w_out[l].astype(BF16),
            'w_ffn_in': w_ffn_in[l].astype(BF16), 'w_ffn_out': w_ffn_out[l].astype(BF16),
        }
        mod = _ada(c_all, w_ada[l], b_ada[l])
        mod_p = mod[:nbp].reshape(nbp, 6, D_MODEL)
        mod_s = mod[nbp:].reshape(nbs, 6, D_MODEL)
        yp, h1, r1, s1 = _block(yp, mod_p, hp, rp, sp, p, l, PROMPT_TILES)
        ys, h2, r2, s2 = _block(ys, mod_s, state_hgrn[l], state_rwkv[l], state_shift[l], p, l, SAMPLE_TILES)
        for lst, val in zip(outs, (h1, r1, s1, h2, r2, s2)):
            lst.append(val)
    stacked = [jnp.stack(o) for o in outs]
    return (yp, ys, *stacked)
```

```python
import functools
import math

import jax
import jax.numpy as jnp
from jax import lax
from jax.experimental import pallas as pl
from jax.experimental.pallas import tpu as pltpu

F32 = jnp.float32
BF16 = jnp.bfloat16

D_MODEL = 1024
HA_HEADS = 4
HA_DK = 128
HA_W = HA_HEADS * HA_DK
RB_HEADS = 8
RB_N = 64
RB_W = RB_HEADS * RB_N
RB_LORA_WA = 128
RB_LORA_G = 128
RB_PROJ_W = 3 * RB_W + RB_LORA_WA + RB_LORA_G
RB_DECAY_SCALE = math.exp(-0.5)
RB_GN_EPS = 64e-5
HG_W = 4 * HA_W
GATE_W = 2 * D_MODEL
PROJ_W = HG_W + RB_PROJ_W + GATE_W
FFN_HIDDEN = 2816
RMS_EPS = 1e-6
LOG2_E = math.log2(math.e)

VMEM_LIMIT_BYTES = 56 * 1024 * 1024
UNITS_PER_ITER = 2
RWKV_GROUP = 64


def _dot(a, b):
    return jnp.dot(a.astype(BF16), b.astype(BF16), preferred_element_type=F32)


def _dot_nt(a, b):
    return lax.dot_general(a.astype(BF16), b.astype(BF16), (((1,), (1,)), ((), ())), preferred_element_type=F32)


def _dot_tn(a, b):
    return lax.dot_general(a.astype(BF16), b.astype(BF16), (((0,), (0,)), ((), ())), preferred_element_type=F32)


def _sigmoid(x):
    return 1.0 / (1.0 + jnp.exp(-x))


def _segsum(x, seg):
    return jnp.dot(x.astype(BF16), seg, preferred_element_type=F32)


def _sum3(m, x):
    hi = x.astype(BF16)
    r1 = x - hi.astype(F32)
    mid = r1.astype(BF16)
    lo = (r1 - mid.astype(F32)).astype(BF16)
    return (jnp.dot(m, hi, preferred_element_type=F32) + jnp.dot(m, mid, preferred_element_type=F32)
            + jnp.dot(m, lo, preferred_element_type=F32))


def _iota(shape, axis):
    return lax.broadcasted_iota(jnp.int32, shape, axis)


def _ada_kernel(c_ref, w_ref, b_ref, o_ref):
    c = c_ref[...]
    o_ref[...] = _dot(c * _sigmoid(c), w_ref[...]) + b_ref[...]


def _ada(c_all, w_ada, b_ada):
    n, d = c_all.shape
    wn = w_ada.shape[1]
    tn = 1024
    return pl.pallas_call(
        _ada_kernel,
        out_shape=jax.ShapeDtypeStruct((n, wn), F32),
        grid=(wn // tn,),
        in_specs=[pl.BlockSpec((n, d), lambda j: (0, 0)),
                  pl.BlockSpec((d, tn), lambda j: (0, j)),
                  pl.BlockSpec((1, tn), lambda j: (0, j))],
        out_specs=pl.BlockSpec((n, tn), lambda j: (0, j)),
        compiler_params=pltpu.CompilerParams(dimension_semantics=("parallel",)),
        name="ada",
    )(c_all, w_ada, b_ada.reshape(1, wn))


def _inproj_kernel(x_ref, mod_ref, g_ref, w_ref, oh_ref, or_ref, og_ref):
    bb, tl, d = x_ref.shape
    x = x_ref[...]
    xn = x * lax.rsqrt(jnp.mean(x * x, axis=-1, keepdims=True) + RMS_EPS) * g_ref[...]
    xm = xn * (1.0 + mod_ref[:, 1:2, :]) + mod_ref[:, 0:1, :]
    xb = xm.reshape(bb * tl, d).astype(BF16)
    lo = 0
    for o_ref in (oh_ref, or_ref, og_ref):
        w = o_ref.shape[-1]
        o_ref[...] = jnp.dot(xb, w_ref[:, lo:lo + w], preferred_element_type=F32).reshape(bb, tl, w)
        lo += w


def _inproj(x, mod, norm_g, w_in_bf16, bb, tl):
    B, L, d = x.shape
    const = lambda b, l: (0, 0)
    tok = lambda b, l: (b, l, 0)
    return pl.pallas_call(
        _inproj_kernel,
        out_shape=(jax.ShapeDtypeStruct((B, L, HG_W), F32),
                   jax.ShapeDtypeStruct((B, L, RB_PROJ_W), F32),
                   jax.ShapeDtypeStruct((B, L, GATE_W), F32)),
        grid=(B // bb, L // tl),
        in_specs=[pl.BlockSpec((bb, tl, d), tok),
                  pl.BlockSpec((bb, 6, d), lambda b, l: (b, 0, 0)),
                  pl.BlockSpec((1, d), const),
                  pl.BlockSpec((d, PROJ_W), const, pipeline_mode=pl.Buffered(1))],
        out_specs=(pl.BlockSpec((bb, tl, HG_W), tok),
                   pl.BlockSpec((bb, tl, RB_PROJ_W), tok),
                   pl.BlockSpec((bb, tl, GATE_W), tok)),
        compiler_params=pltpu.CompilerParams(dimension_semantics=("parallel", "parallel"),
                                             vmem_limit_bytes=VMEM_LIMIT_BYTES),
        name="inproj",
    )(x, mod, norm_g.reshape(1, d), w_in_bf16)


def _hgrn_kernel(ph_ref, s0_ref, lbp_ref, ng_ref, o_ref, s_ref, c2_s, *, layer, bb, nch, C, c):
    @pl.when(pl.program_id(1) == 0)
    def _init():
        s_ref[...] = s0_ref[...]

    hb = lbp_ref[...]
    e = jnp.exp(hb - jnp.max(hb, axis=0, keepdims=True))
    lb = jnp.sum(e[0:layer + 1], axis=0, keepdims=True) / jnp.sum(e, axis=0, keepdims=True)
    ng = ng_ref[...]
    cum_m = (_iota((C, C), 0) >= _iota((C, C), 1)).astype(BF16)
    nsub = C // c
    causal = _iota((C, C), 0) >= _iota((C, C), 1)

    def one_chunk(ci):
        r0 = pl.multiple_of(ci * C, C)
        tasks, units = [], []
        for u in range(bb):
            ph = ph_ref[u, pl.ds(r0, C), :]
            hq = ph[:, 0:HA_W]
            hf = ph[:, HA_W:2 * HA_W]
            hi = ph[:, 2 * HA_W:3 * HA_W]
            hg = ph[:, 3 * HA_W:4 * HA_W]
            q = hq * _sigmoid(hq)
            sg = _sigmoid(hf)
            logf = jnp.log(lb + (1.0 - lb) * sg)
            b2 = _sum3(cum_m, logf) * LOG2_E
            c2 = b2 - jnp.log2((1.0 - lb) * (1.0 - sg))
            c2_s[u] = c2
            unit = dict(u=u, gate=hg * _sigmoid(hg), tasks=[])
            for h in range(HA_HEADS):
                sl = slice(h * HA_DK, (h + 1) * HA_DK)
                t = dict(u=u, h=h, q=q[:, sl], v=hi[:, sl], b2=b2[:, sl], c2=c2[:, sl])
                tasks.append(t)
                unit['tasks'].append(t)
            units.append(unit)
        for t in tasks:
            t['S'] = s_ref[t['u'], t['h']]
        for t in tasks:
            t['o'] = _dot(t['q'] * jnp.exp2(t['b2']), t['S'])
        lane_is = [_iota((c, C), 1) == j for j in range(C)]
        for t in tasks:
            q, b2, c2 = t['q'], t['b2'], t['c2']
            sl = slice(t['h'] * HA_DK, (t['h'] + 1) * HA_DK)
            rows = []
            for i in range(nsub):
                lo = i * c
                qi, bi = q[lo:lo + c], b2[lo:lo + c]
                if i > 0:
                    bref = b2[lo - 1:lo, :]
                    kt = jnp.concatenate([jnp.exp2(bref - c2[:lo]), jnp.zeros((C - lo, HA_DK), F32)], axis=0)
                    sc = _dot_nt(qi * jnp.exp2(bi - bref), kt)
                else:
                    sc = jnp.zeros((c, C), F32)
                for s in range(c):
                    col = jnp.sum(qi * jnp.exp2(bi - c2_s[t['u'], lo + s:lo + s + 1, sl]), axis=-1, keepdims=True)
                    sc = jnp.where(lane_is[lo + s], col, sc)
                rows.append(sc)
            t['scores'] = jnp.where(causal, jnp.concatenate(rows, axis=0), 0.0)
        for t in tasks:
            t['o'] = t['o'] + _dot(t['scores'], t['v'])
        for t in tasks:
            bl = t['b2'][C - 1:C, :]
            kd = jnp.exp2(bl - t['c2'])
            dcol = jnp.transpose(jnp.broadcast_to(jnp.exp2(bl), (8, HA_DK)))[:, 0:1]
            s_ref[t['u'], t['h']] = t['S'] * dcol + _dot_tn(kd, t['v'])
        for unit in units:
            outs = [t['o'] * lax.rsqrt(jnp.mean(t['o'] * t['o'], axis=-1, keepdims=True) + RMS_EPS)
                    for t in unit['tasks']]
            o_ref[unit['u'], pl.ds(r0, C), :] = jnp.concatenate(outs, axis=1) * ng * unit['gate']

    def per_chunk(ci, carry):
        one_chunk(ci)
        return carry

    lax.fori_loop(0, nch, per_chunk, 0)


def _hgrn(ph, s0, lower_bounds, norm_g, layer, bb, tl, C, c):
    B, L, _ = ph.shape
    const = lambda b, l: (0, 0)
    st = lambda b, l: (b, 0, 0, 0)
    kern = functools.partial(_hgrn_kernel, layer=layer, bb=bb, nch=tl // C, C=C, c=c)
    return pl.pallas_call(
        kern,
        out_shape=(jax.ShapeDtypeStruct((B, L, HA_W), F32),
                   jax.ShapeDtypeStruct(s0.shape, F32)),
        grid=(B // bb, L // tl),
        in_specs=[pl.BlockSpec((bb, tl, HG_W), lambda b, l: (b, l, 0)),
                  pl.BlockSpec((bb, HA_HEADS, HA_DK, HA_DK), st),
                  pl.BlockSpec(lower_bounds.shape, const),
                  pl.BlockSpec((1, HA_W), const)],
        out_specs=(pl.BlockSpec((bb, tl, HA_W), lambda b, l: (b, l, 0)),
                   pl.BlockSpec((bb, HA_HEADS, HA_DK, HA_DK), st)),
        scratch_shapes=[pltpu.VMEM((bb, C, HA_W), F32)],
        compiler_params=pltpu.CompilerParams(dimension_semantics=("parallel", "arbitrary"),
                                             vmem_limit_bytes=VMEM_LIMIT_BYTES),
        name="hgrn",
    )(ph, s0, lower_bounds, norm_g.reshape(1, HA_W))


def _rwkv_kernel(pr_ref, prev_ref, sh0_ref, s0_ref, mu_ref, w0_ref, a0_ref, wwa_ref, g2_ref, kk_ref, ka_ref,
                 rk_ref, gnw_ref, gnb_ref, o_ref, s_ref, s2_s, *, bb, tl, C):
    l = pl.program_id(1)
    G = RWKV_GROUP
    nb = G // C
    npair = RB_HEADS // 2
    W = RB_PROJ_W

    @pl.when(l == 0)
    def _init():
        z = jnp.zeros((bb, RB_N, RB_N), F32)
        for p in range(npair):
            top = jnp.concatenate([s0_ref[:, 2 * p], z], axis=2)
            bot = jnp.concatenate([z, s0_ref[:, 2 * p + 1]], axis=2)
            s2_s[:, p] = jnp.concatenate([top, bot], axis=1)

    rg, cg = _iota((G, G), 0), _iota((G, G), 1)
    same_chunk = (rg // C) == (cg // C)
    cum_m = (same_chunk & (rg >= cg)).astype(BF16)
    if nb > 1:
        cum_m = jnp.concatenate([cum_m, same_chunk.astype(BF16)], axis=0)
    rp, cp = _iota((G, 2 * RB_N), 0), _iota((G, 2 * RB_N), 1) % RB_N
    same_c = (rp // C) == (cp // C)
    tri_s = same_c & ((rp % C) > (cp % C))
    tri_i = same_c & ((rp % C) >= (cp % C))
    head0 = _iota((G, 2 * RB_N), 1) < RB_N
    same_head = (_iota((2 * RB_N, 2 * RB_N), 0) // RB_N) == (_iota((2 * RB_N, 2 * RB_N), 1) // RB_N)
    seg = ((_iota((RB_W, RB_W), 0) // RB_N) == (_iota((RB_W, RB_W), 1) // RB_N)).astype(BF16)
    is_tanh = _iota((G, RB_LORA_WA), 1) < RB_LORA_WA // 2
    nsq = int(math.log2(C)) - 1

    def stack2(x):
        return jnp.concatenate([jnp.where(head0, x, 0.0), jnp.where(head0, 0.0, x)], axis=0)

    if tl >= G:
        n_iter, n_unit = tl // G, bb
    else:
        n_unit = min(UNITS_PER_ITER, bb * tl // G)
        n_iter = bb * tl // (G * n_unit)
    sls = [slice(p * 2 * RB_N, (p + 1) * 2 * RB_N) for p in range(npair)]

    def load_rows(it, u):
        if tl >= G:
            r0 = pl.multiple_of(it * G, G)
            x = pr_ref[u, pl.ds(r0, G), :]
            before = pr_ref[u, pl.ds(jnp.maximum(r0 - 1, 0), 1), :]
            head = jnp.where(l == 0, sh0_ref[u], prev_ref[u, 7:8, :])
            first = jnp.where(it == 0, head, before)
            prev = jnp.where(_iota((G, W), 0) == 0, first, pltpu.roll(x, 1, 0))
            return x, prev, [u], (u, pl.ds(r0, G))
        b0 = pl.multiple_of((it * n_unit + u) * nb, nb)
        x = pr_ref[pl.ds(b0, nb)]
        rolled = pltpu.roll(x.reshape(G, W), 1, 0).reshape(nb, tl, W)
        prev = jnp.where(_iota((nb, tl, W), 1) == 0, sh0_ref[pl.ds(b0, nb)], rolled)
        return x.reshape(G, W), prev.reshape(G, W), [b0 + j for j in range(nb)], (pl.ds(b0, nb),)

    def step(it, carry):
        units = []
        for u in range(n_unit):
            x, xprev, bs, out_idx = load_rows(it, u)
            pm = x + (xprev - x) * mu_ref[...]
            dwa = pm[:, 3 * RB_W:3 * RB_W + RB_LORA_WA]
            units.append(dict(bs=bs, out_idx=out_idx, r=pm[:, 0:RB_W], k=pm[:, RB_W:2 * RB_W],
                              v=pm[:, 2 * RB_W:3 * RB_W], dwa=jnp.where(is_tanh, jnp.tanh(dwa), dwa),
                              sdg=_sigmoid(pm[:, 3 * RB_W + RB_LORA_WA:])))
        wa = _dot(jnp.concatenate([q['dwa'] for q in units], axis=0), wwa_ref[...])
        gate = _dot(jnp.concatenate([q['sdg'] for q in units], axis=0), g2_ref[...])
        for i, q in enumerate(units):
            rs = slice(i * G, (i + 1) * G)
            q['lw'] = -RB_DECAY_SCALE * _sigmoid(w0_ref[...] + wa[rs, 0:RB_W])
            q['a'] = _sigmoid(a0_ref[...] + wa[rs, RB_W:])
            q['gate'] = gate[rs]
            q['kk'] = q['k'] * kk_ref[...]
            q['kp'] = q['k'] * (1.0 + (q['a'] - 1.0) * ka_ref[...])
        sums = _segsum(jnp.concatenate([q['kk'] * q['kk'] for q in units]
                                       + [q['r'] * q['kp'] * rk_ref[...] for q in units], axis=0), seg)
        tasks = []
        for i, q in enumerate(units):
            kk = q['kk'] * lax.rsqrt(jnp.maximum(sums[i * G:(i + 1) * G], 1e-24))
            q['bonus'] = sums[(n_unit + i) * G:(n_unit + i + 1) * G] * q['v']
            kb = kk * q['a']
            lw, kp = q['lw'], q['kp']
            gs = _sum3(cum_m, lw)
            g = gs[:G]
            gl = gs[G:] if nb > 1 else g[G - 1:G, :]
            eg, eng, egl = jnp.exp(g), jnp.exp(-g), jnp.exp(gl)
            At = -kk * (eg * jnp.exp(-lw))
            Bt = kb * eng
            Kt = kp * eng
            Rt = q['r'] * eg
            dec = egl * eng
            Ke = kp * dec
            Be = kb * dec
            q['tasks'] = []
            for p in range(npair):
                sl = sls[p]
                t = dict(p=p, bs=q['bs'], AR=jnp.concatenate([At[:, sl], Rt[:, sl]], axis=0), B=Bt[:, sl],
                         K=Kt[:, sl], V=q['v'][:, sl], Ke=Ke[:, sl], Be=Be[:, sl], egl=egl[:, sl])
                tasks.append(t)
                q['tasks'].append(t)
        for t in tasks:
            t['S2'] = [s2_s[b, t['p']] for b in t['bs']]
        for t in tasks:
            t['M'] = _dot_nt(t['AR'], jnp.concatenate([stack2(t['B']), stack2(t['K'])], axis=0))
        for t in tasks:
            if nb == 1:
                x0 = _dot_nt(t['AR'], t['S2'][0])
                t['x0a'], t['x0r'] = x0[:G], x0[G:]
            else:
                x0 = [_dot_nt(jnp.concatenate([t['AR'][j * C:(j + 1) * C], t['AR'][G + j * C:G + (j + 1) * C]],
                                              axis=0), t['S2'][j]) for j in range(nb)]
                t['x0a'] = jnp.concatenate([x[:C] for x in x0], axis=0)
                t['x0r'] = jnp.concatenate([x[C:] for x in x0], axis=0)
        for t in tasks:
            M = t['M']
            lak = jnp.where(tri_s, M[:G, 2 * RB_N:], 0.0)
            lrk = jnp.where(tri_i, M[G:, 2 * RB_N:], 0.0)
            lv = _dot(jnp.concatenate([lak, lrk], axis=0), stack2(t['V']))
            t['X'] = t['x0a'] + lv[:G]
            t['Yv'] = t['x0r'] + lv[G:]
            t['P'] = jnp.where(tri_s, M[:G, :2 * RB_N], 0.0)
        for t in tasks:
            t['SA'] = t['X'] + _dot(t['P'], stack2(t['X']))
        for _ in range(nsq):
            for t in tasks:
                t['P'] = _dot(t['P'], stack2(t['P']))
            for t in tasks:
                t['SA'] = t['SA'] + _dot(t['P'], stack2(t['SA']))
        for t in tasks:
            lrb = jnp.where(tri_i, t['M'][G:, :2 * RB_N], 0.0)
            t['y'] = t['Yv'] + _dot(lrb, stack2(t['SA']))
        for t in tasks:
            for j in range(nb):
                js = slice(j * C, (j + 1) * C)
                upd = _dot_tn(jnp.concatenate([t['V'][js], t['SA'][js]], axis=0),
                              jnp.concatenate([t['Ke'][js], t['Be'][js]], axis=0))
                s2_s[t['bs'][j], t['p']] = t['S2'][j] * t['egl'][j * C:j * C + 1, :] + jnp.where(same_head, upd, 0.0)
        y = jnp.concatenate([jnp.concatenate([t['y'] for t in q['tasks']], axis=1) for q in units], axis=0)
        yc = y - _segsum(y, seg) * (1.0 / RB_N)
        var = _segsum(yc * yc, seg) * (1.0 / RB_N)
        yn = yc * lax.rsqrt(var + RB_GN_EPS) * gnw_ref[...] + gnb_ref[...]
        for i, q in enumerate(units):
            out = (yn[i * G:(i + 1) * G] + q['bonus']) * q['gate']
            if tl >= G:
                o_ref[q['out_idx']] = out
            else:
                o_ref[q['out_idx']] = out.reshape(nb, tl, RB_W)
        return carry

    lax.fori_loop(0, n_iter, step, 0)

    @pl.when(l == pl.num_programs(1) - 1)
    def _fin():
        for p in range(npair):
            S2 = s2_s[:, p]
            s_ref[:, 2 * p] = S2[:, :RB_N, :RB_N]
            s_ref[:, 2 * p + 1] = S2[:, RB_N:, RB_N:]


def _rwkv(pr, shift0, s0, mu, w0, a0, wwa, g2, k_k, k_a, r_k, gn_w, gn_b, bb, tl, C):
    B, L, _ = pr.shape
    const = lambda b, l: (0, 0)
    tok = lambda b, l: (b, l, 0)
    st = lambda b, l: (b, 0, 0, 0)
    vec = lambda x: x.reshape(1, -1)
    kern = functools.partial(_rwkv_kernel, bb=bb, tl=tl, C=C)
    return pl.pallas_call(
        kern,
        out_shape=(jax.ShapeDtypeStruct((B, L, RB_W), F32),
                   jax.ShapeDtypeStruct(s0.shape, F32)),
        grid=(B // bb, L // tl),
        in_specs=[pl.BlockSpec((bb, tl, RB_PROJ_W), tok),
                  pl.BlockSpec((bb, 8, RB_PROJ_W), lambda b, l: (b, jnp.maximum(l * (tl // 8) - 1, 0), 0)),
                  pl.BlockSpec((bb, 1, RB_PROJ_W), lambda b, l: (b, 0, 0)),
                  pl.BlockSpec((bb, RB_HEADS, RB_N, RB_N), st),
                  pl.BlockSpec((1, RB_PROJ_W), const),
                  pl.BlockSpec((1, RB_W), const),
                  pl.BlockSpec((1, RB_W), const),
                  pl.BlockSpec((RB_LORA_WA, 2 * RB_W), const),
                  pl.BlockSpec((RB_LORA_G, RB_W), const),
                  pl.BlockSpec((1, RB_W), const),
                  pl.BlockSpec((1, RB_W), const),
                  pl.BlockSpec((1, RB_W), const),
                  pl.BlockSpec((1, RB_W), const),
                  pl.BlockSpec((1, RB_W), const)],
        out_specs=(pl.BlockSpec((bb, tl, RB_W), tok),
                   pl.BlockSpec((bb, RB_HEADS, RB_N, RB_N), st)),
        scratch_shapes=[pltpu.VMEM((bb, RB_HEADS // 2, 2 * RB_N, 2 * RB_N), F32)],
        compiler_params=pltpu.CompilerParams(dimension_semantics=("parallel", "arbitrary"),
                                             vmem_limit_bytes=VMEM_LIMIT_BYTES),
        name="rwkv",
    )(pr, pr, shift0.reshape(B, 1, RB_PROJ_W), s0, vec(mu), vec(w0), vec(a0), wwa, g2, vec(k_k), vec(k_a),
      vec(r_k), vec(gn_w), vec(gn_b))


def _mixffn_kernel(oa_ref, ob_ref, pg_ref, x_ref, mod_ref, nf_ref, nfin_ref, wua_ref, wub_ref, wo_ref, wfi_ref,
                   wfo_ref, y_ref):
    bb, tl, d = x_ref.shape
    R = bb * tl
    oa = oa_ref[...].reshape(R, HA_W)
    ob = ob_ref[...].reshape(R, RB_W)
    pg = pg_ref[...].reshape(R, GATE_W)
    merged = _sigmoid(pg[:, :d]) * _dot(oa, wua_ref[...]) + _sigmoid(pg[:, d:]) * _dot(ob, wub_ref[...])
    mix = _dot(merged, wo_ref[...]).reshape(bb, tl, d)
    x1 = x_ref[...] + mod_ref[:, 2:3, :] * mix
    xn = x1 * lax.rsqrt(jnp.mean(x1 * x1, axis=-1, keepdims=True) + RMS_EPS) * nf_ref[...]
    xf = (xn * (1.0 + mod_ref[:, 4:5, :]) + mod_ref[:, 3:4, :]).reshape(R, d)
    h = _dot(xf, wfi_ref[...])
    gate, up = h[:, :FFN_HIDDEN], h[:, FFN_HIDDEN:]
    act = gate * _sigmoid(gate) * up
    x2 = x1 + mod_ref[:, 5:6, :] * _dot(act, wfo_ref[...]).reshape(bb, tl, d)
    y_ref[...] = x2 * lax.rsqrt(jnp.mean(x2 * x2, axis=-1, keepdims=True) + RMS_EPS) * nfin_ref[...]


def _mixffn(oa, ob, pg, x, mod, norm_ffn, norm_final, wua, wub, wo, wfi, wfo, bb, tl):
    B, L, d = x.shape
    const = lambda b, l: (0, 0)
    tok = lambda b, l: (b, l, 0)
    wspec = lambda w: pl.BlockSpec(w.shape, const, pipeline_mode=pl.Buffered(1))
    return pl.pallas_call(
        _mixffn_kernel,
        out_shape=jax.ShapeDtypeStruct((B, L, d), F32),
        grid=(B // bb, L // tl),
        in_specs=[pl.BlockSpec((bb, tl, HA_W), tok),
                  pl.BlockSpec((bb, tl, RB_W), tok),
                  pl.BlockSpec((bb, tl, GATE_W), tok),
                  pl.BlockSpec((bb, tl, d), tok),
                  pl.BlockSpec((bb, 6, d), lambda b, l: (b, 0, 0)),
                  pl.BlockSpec((1, d), const),
                  pl.BlockSpec((1, d), const),
                  wspec(wua), wspec(wub), wspec(wo), wspec(wfi), wspec(wfo)],
        out_specs=pl.BlockSpec((bb, tl, d), tok),
        compiler_params=pltpu.CompilerParams(dimension_semantics=("parallel", "parallel"),
                                             vmem_limit_bytes=VMEM_LIMIT_BYTES),
        name="mixffn",
    )(oa, ob, pg, x, mod, norm_ffn.reshape(1, d), norm_final.reshape(1, d), wua, wub, wo, wfi, wfo)


def _block(x, mod, s_h, s_r, shift, p, layer, tiles):
    ph, pr, pg = _inproj(x, mod, p['norm_mix'], p['w_in'], *tiles['dense'])
    oa, s_h = _hgrn(ph, s_h, p['lower_bounds'], p['hgrn_norm'], layer, *tiles['hgrn'])
    ob, s_r = _rwkv(pr, shift, s_r, p['mu'], p['w0'], p['a0'], p['wwa'], p['g2'], p['k_k'], p['k_a'], p['r_k'],
                    p['gn_w'], p['gn_b'], *tiles['rwkv'])
    y = _mixffn(oa, ob, pg, x, mod, p['norm_ffn'], p['norm_final'], p['w_up_a'], p['w_up_b'], p['w_out'],
                p['w_ffn_in'], p['w_ffn_out'], *tiles['dense'])
    return y, s_h, s_r, pr[:, -1]


PROMPT_TILES = {'dense': (1, 512), 'hgrn': (8, 256, 64, 8), 'rwkv': (4, 256, 64)}
SAMPLE_TILES = {'dense': (32, 8), 'hgrn': (16, 8, 8, 8), 'rwkv': (16, 8, 8)}


def kernel(x_prompt, x_sample, c_prompt, c_sample, state_hgrn, state_rwkv, state_shift, w_ada, b_ada, norm_mix, norm_ffn, w_in, hgrn_lower_bounds, hgrn_norm, rwkv_mu, rwkv_w0, rwkv_w2, rwkv_a0, rwkv_a2, rwkv_g2, rwkv_k_k, rwkv_k_a, rwkv_r_k, rwkv_gn_w, rwkv_gn_b, w_up_a, w_up_b, w_out, w_ffn_in, w_ffn_out, norm_final):
    depth = w_ada.shape[0]
    assert depth == 1, "the final norm is fused into the layer kernel"
    nbp, nbs = x_prompt.shape[0], x_sample.shape[0]
    c_all = jnp.concatenate([c_prompt, c_sample], axis=0)

    hp = jnp.zeros((nbp, HA_HEADS, HA_DK, HA_DK), F32)
    rp = jnp.zeros((nbp, RB_HEADS, RB_N, RB_N), F32)
    sp = jnp.zeros((nbp, RB_PROJ_W), x_prompt.dtype)

    yp, ys = x_prompt, x_sample
    outs = [[] for _ in range(6)]
    for l in range(depth):
        half = RB_LORA_WA // 2
        zero = jnp.zeros((half, RB_W), F32)
        wwa = jnp.concatenate([jnp.concatenate([rwkv_w2[l], zero], axis=1),
                               jnp.concatenate([zero, rwkv_a2[l]], axis=1)], axis=0)
        p = {
            'norm_mix': norm_mix[l], 'norm_ffn': norm_ffn[l], 'norm_final': norm_final,
            'w_in': w_in[l].astype(BF16), 'lower_bounds': hgrn_lower_bounds, 'hgrn_norm': hgrn_norm[l],
            'mu': rwkv_mu[l], 'w0': rwkv_w0[l], 'a0': rwkv_a0[l], 'wwa': wwa.astype(BF16),
            'g2': rwkv_g2[l].astype(BF16), 'k_k': rwkv_k_k[l], 'k_a': rwkv_k_a[l], 'r_k': rwkv_r_k[l],
            'gn_w': rwkv_gn_w[l], 'gn_b': rwkv_gn_b[l], 'w_up_a': w_up_a[l].astype(BF16),
            'w_up_b': w_up_b[l].astype(BF16), 'w_out': w_out[l].astype(BF16),
            'w_ffn_in': w_ffn_in[l].astype(BF16), 'w_ffn_out': w_ffn_out[l].astype(BF16),
        }
        mod = _ada(c_all, w_ada[l], b_ada[l])
        mod_p = mod[:nbp].reshape(nbp, 6, D_MODEL)
        mod_s = mod[nbp:].reshape(nbs, 6, D_MODEL)
        yp, h1, r1, s1 = _block(yp, mod_p, hp, rp, sp, p, l, PROMPT_TILES)
        ys, h2, r2, s2 = _block(ys, mod_s, state_hgrn[l], state_rwkv[l], state_shift[l], p, l, SAMPLE_TILES)
        for lst, val in zip(outs, (h1, r1, s1, h2, r2, s2)):
            lst.append(val)
    stacked = [jnp.stack(o) for o in outs]
    return (yp, ys, *stacked)
```

```python
import functools
import math

import jax
import jax.numpy as jnp
from jax import lax
from jax.experimental import pallas as pl
from jax.experimental.pallas import tpu as pltpu

F32 = jnp.float32
BF16 = jnp.bfloat16

D_MODEL = 1024
HA_HEADS = 4
HA_DK = 128
HA_W = HA_HEADS * HA_DK
RB_HEADS = 8
RB_N = 64
RB_W = RB_HEADS * RB_N
RB_LORA_WA = 128
RB_LORA_G = 128
RB_PROJ_W = 3 * RB_W + RB_LORA_WA + RB_LORA_G
RB_DECAY_SCALE = math.exp(-0.5)
RB_GN_EPS = 64e-5
HG_W = 4 * HA_W
GATE_W = 2 * D_MODEL
PROJ_W = HG_W + RB_PROJ_W + GATE_W
FFN_HIDDEN = 2816
RMS_EPS = 1e-6
LOG2_E = math.log2(math.e)

MXU_W = 256
VMEM_LIMIT_BYTES = 56 * 1024 * 1024
UNITS_PER_ITER = 2
RWKV_GROUP = 64


def _dot(a, b):
    return jnp.dot(a.astype(BF16), b.astype(BF16), preferred_element_type=F32)


def _dot_nt(a, b):
    return lax.dot_general(a.astype(BF16), b.astype(BF16), (((1,), (1,)), ((), ())), preferred_element_type=F32)


def _dot_tn(a, b):
    return lax.dot_general(a.astype(BF16), b.astype(BF16), (((0,), (0,)), ((), ())), preferred_element_type=F32)


def _sigmoid(x):
    return 1.0 / (1.0 + jnp.exp(-x))


def _segsum(x, seg):
    w = seg.shape[0]
    xb = x.astype(BF16)
    return jnp.concatenate([jnp.dot(xb[:, i:i + w], seg, preferred_element_type=F32)
                            for i in range(0, x.shape[1], w)], axis=1)


def _sum3(m, x):
    hi = x.astype(BF16)
    r1 = x - hi.astype(F32)
    mid = r1.astype(BF16)
    lo = (r1 - mid.astype(F32)).astype(BF16)
    return (jnp.dot(m, hi, preferred_element_type=F32) + jnp.dot(m, mid, preferred_element_type=F32)
            + jnp.dot(m, lo, preferred_element_type=F32))


def _iota(shape, axis):
    return lax.broadcasted_iota(jnp.int32, shape, axis)


def _ada_kernel(c_ref, w_ref, b_ref, o_ref):
    c = c_ref[...]
    val = _dot(c * _sigmoid(c), w_ref[...]) + b_ref[...]
    for j in range(o_ref.shape[1]):
        @pl.when(pl.program_id(0) == j)
        def _store():
            o_ref[:, j, :] = val


def _ada(c_all, w_ada, b_ada):
    n, d = c_all.shape
    nmod = w_ada.shape[1] // d
    return pl.pallas_call(
        _ada_kernel,
        out_shape=jax.ShapeDtypeStruct((n, nmod, d), F32),
        grid=(nmod,),
        in_specs=[pl.BlockSpec((n, d), lambda j: (0, 0)),
                  pl.BlockSpec((d, d), lambda j: (0, j)),
                  pl.BlockSpec((1, d), lambda j: (0, j))],
        out_specs=pl.BlockSpec((n, nmod, d), lambda j: (0, 0, 0)),
        compiler_params=pltpu.CompilerParams(dimension_semantics=("arbitrary",)),
        name="ada",
    )(c_all, w_ada, b_ada.reshape(1, nmod * d))


def _inproj_kernel(x_ref, mod_ref, g_ref, w_ref, oh_ref, or_ref, og_ref):
    bb, tl, d = x_ref.shape
    x = x_ref[...]
    xn = x * lax.rsqrt(jnp.mean(x * x, axis=-1, keepdims=True) + RMS_EPS) * g_ref[...]
    xm = xn * (1.0 + mod_ref[:, 1:2, :]) + mod_ref[:, 0:1, :]
    xb = xm.reshape(bb * tl, d).astype(BF16)
    lo = 0
    for o_ref in (oh_ref, or_ref, og_ref):
        w = o_ref.shape[-1]
        o_ref[...] = jnp.dot(xb, w_ref[:, lo:lo + w], preferred_element_type=F32).reshape(bb, tl, w)
        lo += w


def _inproj(x, mod, mod_off, norm_g, w_in_bf16, bb, tl):
    B, L, d = x.shape
    mod_blk = mod_off // bb
    const = lambda b, l: (0, 0)
    tok = lambda b, l: (b, l, 0)
    return pl.pallas_call(
        _inproj_kernel,
        out_shape=(jax.ShapeDtypeStruct((B, L, HG_W), F32),
                   jax.ShapeDtypeStruct((B, L, RB_PROJ_W), F32),
                   jax.ShapeDtypeStruct((B, L, GATE_W), F32)),
        grid=(B // bb, L // tl),
        in_specs=[pl.BlockSpec((bb, tl, d), tok),
                  pl.BlockSpec((bb, 6, d), lambda b, l: (b + mod_blk, 0, 0)),
                  pl.BlockSpec((1, d), const),
                  pl.BlockSpec((d, PROJ_W), const, pipeline_mode=pl.Buffered(1))],
        out_specs=(pl.BlockSpec((bb, tl, HG_W), tok),
                   pl.BlockSpec((bb, tl, RB_PROJ_W), tok),
                   pl.BlockSpec((bb, tl, GATE_W), tok)),
        compiler_params=pltpu.CompilerParams(dimension_semantics=("parallel", "parallel"),
                                             vmem_limit_bytes=VMEM_LIMIT_BYTES),
        name="inproj",
    )(x, mod, norm_g.reshape(1, d), w_in_bf16)


def _hgrn_kernel(ph_ref, s0_ref, lbp_ref, ng_ref, o_ref, s_ref, c2_s, *, layer, bb, nch, C, c):
    @pl.when(pl.program_id(1) == 0)
    def _init():
        s_ref[...] = s0_ref[...]

    hb = lbp_ref[...]
    e = jnp.exp(hb - jnp.max(hb, axis=0, keepdims=True))
    lb = jnp.sum(e[0:layer + 1], axis=0, keepdims=True) / jnp.sum(e, axis=0, keepdims=True)
    ng = ng_ref[...]
    cum_m = (_iota((C, C), 0) >= _iota((C, C), 1)).astype(BF16)
    nsub = C // c
    causal = _iota((C, C), 0) >= _iota((C, C), 1)

    def one_chunk(ci):
        r0 = pl.multiple_of(ci * C, C)
        tasks, units = [], []
        for u in range(bb):
            ph = ph_ref[u, pl.ds(r0, C), :]
            hq = ph[:, 0:HA_W]
            hf = ph[:, HA_W:2 * HA_W]
            hi = ph[:, 2 * HA_W:3 * HA_W]
            hg = ph[:, 3 * HA_W:4 * HA_W]
            q = hq * _sigmoid(hq)
            sg = _sigmoid(hf)
            logf = jnp.log(lb + (1.0 - lb) * sg)
            b2 = _sum3(cum_m, logf) * LOG2_E
            c2 = b2 - jnp.log2((1.0 - lb) * (1.0 - sg))
            c2_s[u] = c2
            unit = dict(u=u, gate=hg * _sigmoid(hg), tasks=[])
            for h in range(HA_HEADS):
                sl = slice(h * HA_DK, (h + 1) * HA_DK)
                t = dict(u=u, h=h, q=q[:, sl], v=hi[:, sl], b2=b2[:, sl], c2=c2[:, sl])
                tasks.append(t)
                unit['tasks'].append(t)
            units.append(unit)
        for t in tasks:
            t['S'] = s_ref[t['u'], t['h']]
        for t in tasks:
            t['o'] = _dot(t['q'] * jnp.exp2(t['b2']), t['S'])
        lane_is = [_iota((c, C), 1) == j for j in range(C)]
        for t in tasks:
            q, b2, c2 = t['q'], t['b2'], t['c2']
            sl = slice(t['h'] * HA_DK, (t['h'] + 1) * HA_DK)
            rows = []
            for i in range(nsub):
                lo = i * c
                qi, bi = q[lo:lo + c], b2[lo:lo + c]
                if i > 0:
                    bref = b2[lo - 1:lo, :]
                    kt = jnp.concatenate([jnp.exp2(bref - c2[:lo]), jnp.zeros((C - lo, HA_DK), F32)], axis=0)
                    sc = _dot_nt(qi * jnp.exp2(bi - bref), kt)
                else:
                    sc = jnp.zeros((c, C), F32)
                for s in range(c):
                    col = jnp.sum(qi * jnp.exp2(bi - c2_s[t['u'], lo + s:lo + s + 1, sl]), axis=-1, keepdims=True)
                    sc = jnp.where(lane_is[lo + s], col, sc)
                rows.append(sc)
            t['scores'] = jnp.where(causal, jnp.concatenate(rows, axis=0), 0.0)
        for t in tasks:
            t['o'] = t['o'] + _dot(t['scores'], t['v'])
        for t in tasks:
            bl = t['b2'][C - 1:C, :]
            kd = jnp.exp2(bl - t['c2'])
            dcol = jnp.transpose(jnp.broadcast_to(jnp.exp2(bl), (8, HA_DK)))[:, 0:1]
            s_ref[t['u'], t['h']] = t['S'] * dcol + _dot_tn(kd, t['v'])
        for unit in units:
            outs = [t['o'] * lax.rsqrt(jnp.mean(t['o'] * t['o'], axis=-1, keepdims=True) + RMS_EPS)
                    for t in unit['tasks']]
            o_ref[unit['u'], pl.ds(r0, C), :] = jnp.concatenate(outs, axis=1) * ng * unit['gate']

    def per_chunk(ci, carry):
        one_chunk(ci)
        return carry

    lax.fori_loop(0, nch, per_chunk, 0)


def _hgrn(ph, s0, lower_bounds, norm_g, layer, bb, tl, C, c):
    B, L, _ = ph.shape
    const = lambda b, l: (0, 0)
    st = lambda b, l: (b, 0, 0, 0)
    kern = functools.partial(_hgrn_kernel, layer=layer, bb=bb, nch=tl // C, C=C, c=c)
    return pl.pallas_call(
        kern,
        out_shape=(jax.ShapeDtypeStruct((B, L, HA_W), F32),
                   jax.ShapeDtypeStruct(s0.shape, F32)),
        grid=(B // bb, L // tl),
        in_specs=[pl.BlockSpec((bb, tl, HG_W), lambda b, l: (b, l, 0)),
                  pl.BlockSpec((bb, HA_HEADS, HA_DK, HA_DK), st),
                  pl.BlockSpec(lower_bounds.shape, const),
                  pl.BlockSpec((1, HA_W), const)],
        out_specs=(pl.BlockSpec((bb, tl, HA_W), lambda b, l: (b, l, 0)),
                   pl.BlockSpec((bb, HA_HEADS, HA_DK, HA_DK), st)),
        scratch_shapes=[pltpu.VMEM((bb, C, HA_W), F32)],
        compiler_params=pltpu.CompilerParams(dimension_semantics=("parallel", "arbitrary"),
                                             vmem_limit_bytes=VMEM_LIMIT_BYTES),
        name="hgrn",
    )(ph, s0, lower_bounds, norm_g.reshape(1, HA_W))


def _rwkv_kernel(pr_ref, prev_ref, sh0_ref, s0_ref, mu_ref, w0_ref, a0_ref, wwa_ref, g2_ref, kk_ref, ka_ref,
                 rk_ref, gnw_ref, gnb_ref, o_ref, s_ref, s2_s, *, bb, tl, C):
    l = pl.program_id(1)
    G = RWKV_GROUP
    nb = G // C
    npair = RB_HEADS // 2
    W = RB_PROJ_W

    @pl.when(l == 0)
    def _init():
        z = jnp.zeros((bb, RB_N, RB_N), F32)
        for p in range(npair):
            top = jnp.concatenate([s0_ref[:, 2 * p], z], axis=2)
            bot = jnp.concatenate([z, s0_ref[:, 2 * p + 1]], axis=2)
            s2_s[:, p] = jnp.concatenate([top, bot], axis=1)

    rg, cg = _iota((G, G), 0), _iota((G, G), 1)
    same_chunk = (rg // C) == (cg // C)
    cum_m = (same_chunk & (rg >= cg)).astype(BF16)
    if nb > 1:
        cum_m = jnp.concatenate([cum_m, same_chunk.astype(BF16)], axis=0)
    rp, cp = _iota((G, 2 * RB_N), 0), _iota((G, 2 * RB_N), 1) % RB_N
    same_c = (rp // C) == (cp // C)
    tri_s = same_c & ((rp % C) > (cp % C))
    tri_i = same_c & ((rp % C) >= (cp % C))
    head0 = _iota((G, 2 * RB_N), 1) < RB_N
    same_head = (_iota((2 * RB_N, 2 * RB_N), 0) // RB_N) == (_iota((2 * RB_N, 2 * RB_N), 1) // RB_N)
    seg = ((_iota((MXU_W, MXU_W), 0) // RB_N) == (_iota((MXU_W, MXU_W), 1) // RB_N)).astype(BF16)
    eye2 = jnp.where(rp == cp, 1.0, 0.0)
    is_tanh = _iota((G, RB_LORA_WA), 1) < RB_LORA_WA // 2
    nsq = int(math.log2(C)) - 1

    def stack2(x):
        return jnp.concatenate([jnp.where(head0, x, 0.0), jnp.where(head0, 0.0, x)], axis=0)

    if tl >= G:
        n_iter, n_unit = tl // G, bb
    else:
        n_unit = min(UNITS_PER_ITER, bb * tl // G)
        n_iter = bb * tl // (G * n_unit)
    sls = [slice(p * 2 * RB_N, (p + 1) * 2 * RB_N) for p in range(npair)]

    def load_rows(it, u):
        if tl >= G:
            r0 = pl.multiple_of(it * G, G)
            x = pr_ref[u, pl.ds(r0, G), :]
            before = pr_ref[u, pl.ds(jnp.maximum(r0 - 1, 0), 1), :]
            head = jnp.where(l == 0, sh0_ref[u], prev_ref[u, 7:8, :])
            first = jnp.where(it == 0, head, before)
            prev = jnp.where(_iota((G, W), 0) == 0, first, pltpu.roll(x, 1, 0))
            return x, prev, [u], (u, pl.ds(r0, G))
        b0 = pl.multiple_of((it * n_unit + u) * nb, nb)
        x = pr_ref[pl.ds(b0, nb)]
        rolled = pltpu.roll(x.reshape(G, W), 1, 0).reshape(nb, tl, W)
        prev = jnp.where(_iota((nb, tl, W), 1) == 0, sh0_ref[pl.ds(b0, nb)], rolled)
        return x.reshape(G, W), prev.reshape(G, W), [b0 + j for j in range(nb)], (pl.ds(b0, nb),)

    def step(it, carry):
        units = []
        for u in range(n_unit):
            x, xprev, bs, out_idx = load_rows(it, u)
            pm = x + (xprev - x) * mu_ref[...]
            dwa = pm[:, 3 * RB_W:3 * RB_W + RB_LORA_WA]
            units.append(dict(bs=bs, out_idx=out_idx, r=pm[:, 0:RB_W], k=pm[:, RB_W:2 * RB_W],
                              v=pm[:, 2 * RB_W:3 * RB_W], dwa=jnp.where(is_tanh, jnp.tanh(dwa), dwa),
                              sdg=_sigmoid(pm[:, 3 * RB_W + RB_LORA_WA:])))
        wa = _dot(jnp.concatenate([q['dwa'] for q in units], axis=0), wwa_ref[...])
        gate = _dot(jnp.concatenate([q['sdg'] for q in units], axis=0), g2_ref[...])
        for i, q in enumerate(units):
            rs = slice(i * G, (i + 1) * G)
            q['lw'] = -RB_DECAY_SCALE * _sigmoid(w0_ref[...] + wa[rs, 0:RB_W])
            q['a'] = _sigmoid(a0_ref[...] + wa[rs, RB_W:])
            q['gate'] = gate[rs]
            q['kk'] = q['k'] * kk_ref[...]
            q['kp'] = q['k'] * (1.0 + (q['a'] - 1.0) * ka_ref[...])
        sums = _segsum(jnp.concatenate([q['kk'] * q['kk'] for q in units]
                                       + [q['r'] * q['kp'] * rk_ref[...] for q in units], axis=0), seg)
        tasks = []
        for i, q in enumerate(units):
            kk = q['kk'] * lax.rsqrt(jnp.maximum(sums[i * G:(i + 1) * G], 1e-24))
            q['bonus'] = sums[(n_unit + i) * G:(n_unit + i + 1) * G] * q['v']
            kb = kk * q['a']
            lw, kp = q['lw'], q['kp']
            gs = _sum3(cum_m, lw)
            g = gs[:G]
            gl = gs[G:] if nb > 1 else g[G - 1:G, :]
            eg, eng, egl = jnp.exp(g), jnp.exp(-g), jnp.exp(gl)
            At = -kk * (eg * jnp.exp(-lw))
            Bt = kb * eng
            Kt = kp * eng
            Rt = q['r'] * eg
            dec = egl * eng
            Ke = kp * dec
            Be = kb * dec
            q['tasks'] = []
            for p in range(npair):
                sl = sls[p]
                t = dict(p=p, bs=q['bs'], AR=jnp.concatenate([At[:, sl], Rt[:, sl]], axis=0), B=Bt[:, sl],
                         K=Kt[:, sl], V=q['v'][:, sl], Ke=Ke[:, sl], Be=Be[:, sl], egl=egl[:, sl])
                tasks.append(t)
                q['tasks'].append(t)
        for t in tasks:
            t['S2'] = [s2_s[b, t['p']] for b in t['bs']]
        for t in tasks:
            t['M'] = _dot_nt(t['AR'], jnp.concatenate([stack2(t['B']), stack2(t['K'])], axis=0))
        for t in tasks:
            if nb == 1:
                x0 = _dot_nt(t['AR'], t['S2'][0])
                t['x0a'], t['x0r'] = x0[:G], x0[G:]
            else:
                x0 = [_dot_nt(jnp.concatenate([t['AR'][j * C:(j + 1) * C], t['AR'][G + j * C:G + (j + 1) * C]],
                                              axis=0), t['S2'][j]) for j in range(nb)]
                t['x0a'] = jnp.concatenate([x[:C] for x in x0], axis=0)
                t['x0r'] = jnp.concatenate([x[C:] for x in x0], axis=0)
        for t in tasks:
            M = t['M']
            lak = jnp.where(tri_s, M[:G, 2 * RB_N:], 0.0)
            lrk = jnp.where(tri_i, M[G:, 2 * RB_N:], 0.0)
            lv = _dot(jnp.concatenate([lak, lrk], axis=0), stack2(t['V']))
            t['X'] = t['x0a'] + lv[:G]
            t['Yv'] = t['x0r'] + lv[G:]
            t['P'] = jnp.where(tri_s, M[:G, :2 * RB_N], 0.0)
        for t in tasks:
            t['T'] = eye2 + t['P']
        for k in range(nsq):
            for t in tasks:
                rhs = stack2(t['P'])
                if k == 0:
                    t['P'] = _dot(t['P'], rhs)
                else:
                    pt = _dot(jnp.concatenate([t['P'], t['T']], axis=0), rhs)
                    t['P'] = pt[:G]
                    t['T'] = t['T'] + pt[G:]
        for t in tasks:
            t['T'] = t['T'] + _dot(t['T'], stack2(t['P']))
        for t in tasks:
            t['SA'] = _dot(t['T'], stack2(t['X']))
        for t in tasks:
            lrb = jnp.where(tri_i, t['M'][G:, :2 * RB_N], 0.0)
            t['y'] = t['Yv'] + _dot(lrb, stack2(t['SA']))
        for t in tasks:
            for j in range(nb):
                js = slice(j * C, (j + 1) * C)
                upd = _dot_tn(jnp.concatenate([t['V'][js], t['SA'][js]], axis=0),
                              jnp.concatenate([t['Ke'][js], t['Be'][js]], axis=0))
                s2_s[t['bs'][j], t['p']] = t['S2'][j] * t['egl'][j * C:j * C + 1, :] + jnp.where(same_head, upd, 0.0)
        y = jnp.concatenate([jnp.concatenate([t['y'] for t in q['tasks']], axis=1) for q in units], axis=0)
        yc = y - _segsum(y, seg) * (1.0 / RB_N)
        var = _segsum(yc * yc, seg) * (1.0 / RB_N)
        yn = yc * lax.rsqrt(var + RB_GN_EPS) * gnw_ref[...] + gnb_ref[...]
        for i, q in enumerate(units):
            out = (yn[i * G:(i + 1) * G] + q['bonus']) * q['gate']
            if tl >= G:
                o_ref[q['out_idx']] = out
            else:
                o_ref[q['out_idx']] = out.reshape(nb, tl, RB_W)
        return carry

    lax.fori_loop(0, n_iter, step, 0)

    @pl.when(l == pl.num_programs(1) - 1)
    def _fin():
        for p in range(npair):
            S2 = s2_s[:, p]
            s_ref[:, 2 * p] = S2[:, :RB_N, :RB_N]
            s_ref[:, 2 * p + 1] = S2[:, RB_N:, RB_N:]


def _rwkv(pr, shift0, s0, mu, w0, a0, wwa, g2, k_k, k_a, r_k, gn_w, gn_b, bb, tl, C):
    B, L, _ = pr.shape
    const = lambda b, l: (0, 0)
    tok = lambda b, l: (b, l, 0)
    st = lambda b, l: (b, 0, 0, 0)
    vec = lambda x: x.reshape(1, -1)
    kern = functools.partial(_rwkv_kernel, bb=bb, tl=tl, C=C)
    return pl.pallas_call(
        kern,
        out_shape=(jax.ShapeDtypeStruct((B, L, RB_W), F32),
                   jax.ShapeDtypeStruct(s0.shape, F32)),
        grid=(B // bb, L // tl),
        in_specs=[pl.BlockSpec((bb, tl, RB_PROJ_W), tok),
                  pl.BlockSpec((bb, 8, RB_PROJ_W), lambda b, l: (b, jnp.maximum(l * (tl // 8) - 1, 0), 0)),
                  pl.BlockSpec((bb, 1, RB_PROJ_W), lambda b, l: (b, 0, 0)),
                  pl.BlockSpec((bb, RB_HEADS, RB_N, RB_N), st),
                  pl.BlockSpec((1, RB_PROJ_W), const),
                  pl.BlockSpec((1, RB_W), const),
                  pl.BlockSpec((1, RB_W), const),
                  pl.BlockSpec((RB_LORA_WA, 2 * RB_W), const),
                  pl.BlockSpec((RB_LORA_G, RB_W), const),
                  pl.BlockSpec((1, RB_W), const),
                  pl.BlockSpec((1, RB_W), const),
                  pl.BlockSpec((1, RB_W), const),
                  pl.BlockSpec((1, RB_W), const),
                  pl.BlockSpec((1, RB_W), const)],
        out_specs=(pl.BlockSpec((bb, tl, RB_W), tok),
                   pl.BlockSpec((bb, RB_HEADS, RB_N, RB_N), st)),
        scratch_shapes=[pltpu.VMEM((bb, RB_HEADS // 2, 2 * RB_N, 2 * RB_N), F32)],
        compiler_params=pltpu.CompilerParams(dimension_semantics=("parallel", "arbitrary"),
                                             vmem_limit_bytes=VMEM_LIMIT_BYTES),
        name="rwkv",
    )(pr, pr, shift0.reshape(B, 1, RB_PROJ_W), s0, vec(mu), vec(w0), vec(a0), wwa, g2, vec(k_k), vec(k_a),
      vec(r_k), vec(gn_w), vec(gn_b))


def _mixffn_kernel(oa_ref, ob_ref, pg_ref, x_ref, mod_ref, nf_ref, nfin_ref, wua_ref, wub_ref, wo_ref, wfi_ref,
                   wfo_ref, y_ref):
    bb, tl, d = x_ref.shape
    R = bb * tl
    oa = oa_ref[...].reshape(R, HA_W)
    ob = ob_ref[...].reshape(R, RB_W)
    pg = pg_ref[...].reshape(R, GATE_W)
    merged = _sigmoid(pg[:, :d]) * _dot(oa, wua_ref[...]) + _sigmoid(pg[:, d:]) * _dot(ob, wub_ref[...])
    mix = _dot(merged, wo_ref[...]).reshape(bb, tl, d)
    x1 = x_ref[...] + mod_ref[:, 2:3, :] * mix
    xn = x1 * lax.rsqrt(jnp.mean(x1 * x1, axis=-1, keepdims=True) + RMS_EPS) * nf_ref[...]
    xf = (xn * (1.0 + mod_ref[:, 4:5, :]) + mod_ref[:, 3:4, :]).reshape(R, d)
    h = _dot(xf, wfi_ref[...])
    gate, up = h[:, :FFN_HIDDEN], h[:, FFN_HIDDEN:]
    act = gate * _sigmoid(gate) * up
    x2 = x1 + mod_ref[:, 5:6, :] * _dot(act, wfo_ref[...]).reshape(bb, tl, d)
    y_ref[...] = x2 * lax.rsqrt(jnp.mean(x2 * x2, axis=-1, keepdims=True) + RMS_EPS) * nfin_ref[...]


def _mixffn(oa, ob, pg, x, mod, mod_off, norm_ffn, norm_final, wua, wub, wo, wfi, wfo, bb, tl):
    B, L, d = x.shape
    mod_blk = mod_off // bb
    const = lambda b, l: (0, 0)
    tok = lambda b, l: (b, l, 0)
    wspec = lambda w: pl.BlockSpec(w.shape, const, pipeline_mode=pl.Buffered(1))
    return pl.pallas_call(
        _mixffn_kernel,
        out_shape=jax.ShapeDtypeStruct((B, L, d), F32),
        grid=(B // bb, L // tl),
        in_specs=[pl.BlockSpec((bb, tl, HA_W), tok),
                  pl.BlockSpec((bb, tl, RB_W), tok),
                  pl.BlockSpec((bb, tl, GATE_W), tok),
                  pl.BlockSpec((bb, tl, d), tok),
                  pl.BlockSpec((bb, 6, d), lambda b, l: (b + mod_blk, 0, 0)),
                  pl.BlockSpec((1, d), const),
                  pl.BlockSpec((1, d), const),
                  wspec(wua), wspec(wub), wspec(wo), wspec(wfi), wspec(wfo)],
        out_specs=pl.BlockSpec((bb, tl, d), tok),
        compiler_params=pltpu.CompilerParams(dimension_semantics=("parallel", "parallel"),
                                             vmem_limit_bytes=VMEM_LIMIT_BYTES),
        name="mixffn",
    )(oa, ob, pg, x, mod, norm_ffn.reshape(1, d), norm_final.reshape(1, d), wua, wub, wo, wfi, wfo)


def _block(x, mod, mod_off, s_h, s_r, shift, p, layer, tiles):
    assert mod_off % tiles['dense'][0] == 0
    ph, pr, pg = _inproj(x, mod, mod_off, p['norm_mix'], p['w_in'], *tiles['dense'])
    oa, s_h = _hgrn(ph, s_h, p['lower_bounds'], p['hgrn_norm'], layer, *tiles['hgrn'])
    ob, s_r = _rwkv(pr, shift, s_r, p['mu'], p['w0'], p['a0'], p['wwa'], p['g2'], p['k_k'], p['k_a'], p['r_k'],
                    p['gn_w'], p['gn_b'], *tiles['rwkv'])
    y = _mixffn(oa, ob, pg, x, mod, mod_off, p['norm_ffn'], p['norm_final'], p['w_up_a'], p['w_up_b'], p['w_out'],
                p['w_ffn_in'], p['w_ffn_out'], *tiles['dense'])
    return y, s_h, s_r, pr[:, -1]


PROMPT_TILES = {'dense': (1, 512), 'hgrn': (8, 128, 64, 8), 'rwkv': (4, 128, 64)}
SAMPLE_TILES = {'dense': (32, 8), 'hgrn': (16, 8, 8, 8), 'rwkv': (16, 8, 8)}


def kernel(x_prompt, x_sample, c_prompt, c_sample, state_hgrn, state_rwkv, state_shift, w_ada, b_ada, norm_mix, norm_ffn, w_in, hgrn_lower_bounds, hgrn_norm, rwkv_mu, rwkv_w0, rwkv_w2, rwkv_a0, rwkv_a2, rwkv_g2, rwkv_k_k, rwkv_k_a, rwkv_r_k, rwkv_gn_w, rwkv_gn_b, w_up_a, w_up_b, w_out, w_ffn_in, w_ffn_out, norm_final):
    depth = w_ada.shape[0]
    assert depth == 1, "the final norm is fused into the layer kernel"
    nbp, nbs = x_prompt.shape[0], x_sample.shape[0]

    hp = jnp.zeros((nbp, HA_HEADS, HA_DK, HA_DK), F32)
    rp = jnp.zeros((nbp, RB_HEADS, RB_N, RB_N), F32)
    sp = jnp.zeros((nbp, RB_PROJ_W), x_prompt.dtype)

    yp, ys = x_prompt, x_sample
    outs = [[] for _ in range(6)]
    for l in range(depth):
        half = RB_LORA_WA // 2
        zero = jnp.zeros((half, RB_W), F32)
        wwa = jnp.concatenate([jnp.concatenate([rwkv_w2[l], zero], axis=1),
                               jnp.concatenate([zero, rwkv_a2[l]], axis=1)], axis=0)
        p = {
            'norm_mix': norm_mix[l], 'norm_ffn': norm_ffn[l], 'norm_final': norm_final,
            'w_in': w_in[l].astype(BF16), 'lower_bounds': hgrn_lower_bounds, 'hgrn_norm': hgrn_norm[l],
            'mu': rwkv_mu[l], 'w0': rwkv_w0[l], 'a0': rwkv_a0[l], 'wwa': wwa.astype(BF16),
            'g2': rwkv_g2[l].astype(BF16), 'k_k': rwkv_k_k[l], 'k_a': rwkv_k_a[l], 'r_k': rwkv_r_k[l],
            'gn_w': rwkv_gn_w[l], 'gn_b': rwkv_gn_b[l], 'w_up_a': w_up_a[l].astype(BF16),
            'w_up_b': w_up_b[l].astype(BF16), 'w_out': w_out[l].astype(BF16),
            'w_ffn_in': w_ffn_in[l].astype(BF16), 'w_ffn_out': w_ffn_out[l].astype(BF16),
        }
        mod = _ada(jnp.concatenate([c_sample, c_prompt], axis=0), w_ada[l], b_ada[l])
        yp, h1, r1, s1 = _block(yp, mod, nbs, hp, rp, sp, p, l, PROMPT_TILES)
        ys, h2, r2, s2 = _block(ys, mod, 0, state_hgrn[l], state_rwkv[l], state_shift[l], p, l, SAMPLE_TILES)
        for lst, val in zip(outs, (h1, r1, s1, h2, r2, s2)):
            lst.append(val)
    stacked = [jnp.stack(o) for o in outs]
    return (yp, ys, *stacked)
```

```python
import functools
import math

import jax
import jax.numpy as jnp
from jax import lax
from jax.experimental import pallas as pl
from jax.experimental.pallas import tpu as pltpu

F32 = jnp.float32
BF16 = jnp.bfloat16

D_MODEL = 1024
HA_HEADS = 4
HA_DK = 128
HA_W = HA_HEADS * HA_DK
RB_HEADS = 8
RB_N = 64
RB_W = RB_HEADS * RB_N
RB_LORA_WA = 128
RB_LORA_G = 128
RB_PROJ_W = 3 * RB_W + RB_LORA_WA + RB_LORA_G
RB_DECAY_SCALE = math.exp(-0.5)
RB_GN_EPS = 64e-5
HG_W = 4 * HA_W
GATE_W = 2 * D_MODEL
PROJ_W = HG_W + RB_PROJ_W + GATE_W
FFN_HIDDEN = 2816
RMS_EPS = 1e-6
LOG2_E = math.log2(math.e)

MXU_W = 256
VMEM_LIMIT_BYTES = 56 * 1024 * 1024
INPROJ_SPLIT = 4
MIXFFN_SPLIT = 2
UNITS_PER_ITER = 2
RWKV_GROUP = 64


def _dot(a, b):
    return jnp.dot(a.astype(BF16), b.astype(BF16), preferred_element_type=F32)


def _dot_nt(a, b):
    return lax.dot_general(a.astype(BF16), b.astype(BF16), (((1,), (1,)), ((), ())), preferred_element_type=F32)


def _dot_tn(a, b):
    return lax.dot_general(a.astype(BF16), b.astype(BF16), (((0,), (0,)), ((), ())), preferred_element_type=F32)


def _sigmoid(x):
    return 1.0 / (1.0 + jnp.exp(-x))


def _segsum(x, seg):
    w = seg.shape[0]
    xb = x.astype(BF16)
    return jnp.concatenate([jnp.dot(xb[:, i:i + w], seg, preferred_element_type=F32)
                            for i in range(0, x.shape[1], w)], axis=1)


def _sum3(m, x):
    hi = x.astype(BF16)
    r1 = x - hi.astype(F32)
    mid = r1.astype(BF16)
    lo = (r1 - mid.astype(F32)).astype(BF16)
    return (jnp.dot(m, hi, preferred_element_type=F32) + jnp.dot(m, mid, preferred_element_type=F32)
            + jnp.dot(m, lo, preferred_element_type=F32))


def _run_staged(gens):
    gens = list(gens)
    while gens:
        for g in list(gens):
            try:
                next(g)
            except StopIteration:
                gens.remove(g)


def _row_parts(part, bb, tl, ns):
    if bb == 1:
        step = tl // ns
        return [part(slice(None), slice(i * step, (i + 1) * step)) for i in range(ns)]
    step = bb // ns
    return [part(slice(i * step, (i + 1) * step), slice(None)) for i in range(ns)]


def _iota(shape, axis):
    return lax.broadcasted_iota(jnp.int32, shape, axis)


def _ada_kernel(c_ref, w_ref, b_ref, o_ref):
    c = c_ref[...]
    val = _dot(c * _sigmoid(c), w_ref[...]) + b_ref[...]
    for j in range(o_ref.shape[1]):
        @pl.when(pl.program_id(0) == j)
        def _store():
            o_ref[:, j, :] = val


def _ada(c_all, w_ada, b_ada):
    n, d = c_all.shape
    nmod = w_ada.shape[1] // d
    return pl.pallas_call(
        _ada_kernel,
        out_shape=jax.ShapeDtypeStruct((n, nmod, d), F32),
        grid=(nmod,),
        in_specs=[pl.BlockSpec((n, d), lambda j: (0, 0)),
                  pl.BlockSpec((d, d), lambda j: (0, j)),
                  pl.BlockSpec((1, d), lambda j: (0, j))],
        out_specs=pl.BlockSpec((n, nmod, d), lambda j: (0, 0, 0)),
        compiler_params=pltpu.CompilerParams(dimension_semantics=("arbitrary",)),
        name="ada",
    )(c_all, w_ada, b_ada.reshape(1, nmod * d))


def _inproj_kernel(x_ref, mod_ref, g_ref, w_ref, oh_ref, or_ref, og_ref):
    bb, tl, d = x_ref.shape

    def part(bsl, tsl):
        x = x_ref[bsl, tsl, :]
        nb_, nt_ = x.shape[0], x.shape[1]
        xn = x * lax.rsqrt(jnp.mean(x * x, axis=-1, keepdims=True) + RMS_EPS) * g_ref[...]
        xm = xn * (1.0 + mod_ref[bsl, 1:2, :]) + mod_ref[bsl, 0:1, :]
        xb = xm.reshape(nb_ * nt_, d).astype(BF16)
        yield
        lo = 0
        for o_ref in (oh_ref, or_ref, og_ref):
            w = o_ref.shape[-1]
            o_ref[bsl, tsl, :] = jnp.dot(xb, w_ref[:, lo:lo + w], preferred_element_type=F32).reshape(nb_, nt_, w)
            lo += w
            yield

    _run_staged(_row_parts(part, bb, tl, INPROJ_SPLIT))


def _inproj(x, mod, mod_off, norm_g, w_in_bf16, bb, tl):
    B, L, d = x.shape
    mod_blk = mod_off // bb
    const = lambda b, l: (0, 0)
    tok = lambda b, l: (b, l, 0)
    return pl.pallas_call(
        _inproj_kernel,
        out_shape=(jax.ShapeDtypeStruct((B, L, HG_W), F32),
                   jax.ShapeDtypeStruct((B, L, RB_PROJ_W), F32),
                   jax.ShapeDtypeStruct((B, L, GATE_W), F32)),
        grid=(B // bb, L // tl),
        in_specs=[pl.BlockSpec((bb, tl, d), tok),
                  pl.BlockSpec((bb, 6, d), lambda b, l: (b + mod_blk, 0, 0)),
                  pl.BlockSpec((1, d), const),
                  pl.BlockSpec((d, PROJ_W), const, pipeline_mode=pl.Buffered(1))],
        out_specs=(pl.BlockSpec((bb, tl, HG_W), tok),
                   pl.BlockSpec((bb, tl, RB_PROJ_W), tok),
                   pl.BlockSpec((bb, tl, GATE_W), tok)),
        compiler_params=pltpu.CompilerParams(dimension_semantics=("parallel", "parallel"),
                                             vmem_limit_bytes=VMEM_LIMIT_BYTES),
        name="inproj",
    )(x, mod, norm_g.reshape(1, d), w_in_bf16)


def _hgrn_kernel(ph_ref, s0_ref, lbp_ref, ng_ref, o_ref, s_ref, c2_s, *, layer, bb, nch, C, c):
    @pl.when(pl.program_id(1) == 0)
    def _init():
        s_ref[...] = s0_ref[...]

    hb = lbp_ref[...]
    e = jnp.exp(hb - jnp.max(hb, axis=0, keepdims=True))
    lb = jnp.sum(e[0:layer + 1], axis=0, keepdims=True) / jnp.sum(e, axis=0, keepdims=True)
    ng = ng_ref[...]
    cum_m = (_iota((C, C), 0) >= _iota((C, C), 1)).astype(BF16)
    nsub = C // c
    causal = _iota((C, C), 0) >= _iota((C, C), 1)

    def one_chunk(ci):
        r0 = pl.multiple_of(ci * C, C)
        tasks, units = [], []
        for u in range(bb):
            ph = ph_ref[u, pl.ds(r0, C), :]
            hq = ph[:, 0:HA_W]
            hf = ph[:, HA_W:2 * HA_W]
            hi = ph[:, 2 * HA_W:3 * HA_W]
            hg = ph[:, 3 * HA_W:4 * HA_W]
            q = hq * _sigmoid(hq)
            sg = _sigmoid(hf)
            logf = jnp.log(lb + (1.0 - lb) * sg)
            b2 = _sum3(cum_m, logf) * LOG2_E
            c2 = b2 - jnp.log2((1.0 - lb) * (1.0 - sg))
            c2_s[u] = c2
            unit = dict(u=u, gate=hg * _sigmoid(hg), tasks=[])
            for h in range(HA_HEADS):
                sl = slice(h * HA_DK, (h + 1) * HA_DK)
                t = dict(u=u, h=h, q=q[:, sl], v=hi[:, sl], b2=b2[:, sl], c2=c2[:, sl])
                tasks.append(t)
                unit['tasks'].append(t)
            units.append(unit)
        for t in tasks:
            t['S'] = s_ref[t['u'], t['h']]
        for t in tasks:
            t['o'] = _dot(t['q'] * jnp.exp2(t['b2']), t['S'])
        lane_is = [_iota((c, C), 1) == j for j in range(C)]
        for t in tasks:
            q, b2, c2 = t['q'], t['b2'], t['c2']
            sl = slice(t['h'] * HA_DK, (t['h'] + 1) * HA_DK)
            rows = []
            for i in range(nsub):
                lo = i * c
                qi, bi = q[lo:lo + c], b2[lo:lo + c]
                if i > 0:
                    bref = b2[lo - 1:lo, :]
                    kt = jnp.concatenate([jnp.exp2(bref - c2[:lo]), jnp.zeros((C - lo, HA_DK), F32)], axis=0)
                    sc = _dot_nt(qi * jnp.exp2(bi - bref), kt)
                else:
                    sc = jnp.zeros((c, C), F32)
                for s in range(c):
                    col = jnp.sum(qi * jnp.exp2(bi - c2_s[t['u'], lo + s:lo + s + 1, sl]), axis=-1, keepdims=True)
                    sc = jnp.where(lane_is[lo + s], col, sc)
                rows.append(sc)
            t['scores'] = jnp.where(causal, jnp.concatenate(rows, axis=0), 0.0)
        for t in tasks:
            t['o'] = t['o'] + _dot(t['scores'], t['v'])
        for t in tasks:
            bl = t['b2'][C - 1:C, :]
            kd = jnp.exp2(bl - t['c2'])
            dcol = jnp.transpose(jnp.broadcast_to(jnp.exp2(bl), (8, HA_DK)))[:, 0:1]
            s_ref[t['u'], t['h']] = t['S'] * dcol + _dot_tn(kd, t['v'])
        for unit in units:
            outs = [t['o'] * lax.rsqrt(jnp.mean(t['o'] * t['o'], axis=-1, keepdims=True) + RMS_EPS)
                    for t in unit['tasks']]
            o_ref[unit['u'], pl.ds(r0, C), :] = jnp.concatenate(outs, axis=1) * ng * unit['gate']

    def per_chunk(ci, carry):
        one_chunk(ci)
        return carry

    lax.fori_loop(0, nch, per_chunk, 0)


def _hgrn(ph, s0, lower_bounds, norm_g, layer, bb, tl, C, c):
    B, L, _ = ph.shape
    const = lambda b, l: (0, 0)
    st = lambda b, l: (b, 0, 0, 0)
    kern = functools.partial(_hgrn_kernel, layer=layer, bb=bb, nch=tl // C, C=C, c=c)
    return pl.pallas_call(
        kern,
        out_shape=(jax.ShapeDtypeStruct((B, L, HA_W), F32),
                   jax.ShapeDtypeStruct(s0.shape, F32)),
        grid=(B // bb, L // tl),
        in_specs=[pl.BlockSpec((bb, tl, HG_W), lambda b, l: (b, l, 0)),
                  pl.BlockSpec((bb, HA_HEADS, HA_DK, HA_DK), st),
                  pl.BlockSpec(lower_bounds.shape, const),
                  pl.BlockSpec((1, HA_W), const)],
        out_specs=(pl.BlockSpec((bb, tl, HA_W), lambda b, l: (b, l, 0)),
                   pl.BlockSpec((bb, HA_HEADS, HA_DK, HA_DK), st)),
        scratch_shapes=[pltpu.VMEM((bb, C, HA_W), F32)],
        compiler_params=pltpu.CompilerParams(dimension_semantics=("parallel", "arbitrary"),
                                             vmem_limit_bytes=VMEM_LIMIT_BYTES),
        name="hgrn",
    )(ph, s0, lower_bounds, norm_g.reshape(1, HA_W))


def _rwkv_kernel(pr_ref, prev_ref, sh0_ref, s0_ref, mu_ref, w0_ref, a0_ref, wwa_ref, g2_ref, kk_ref, ka_ref,
                 rk_ref, gnw_ref, gnb_ref, o_ref, s_ref, sh_ref, s2_s, *, bb, tl, C):
    l = pl.program_id(1)
    G = RWKV_GROUP
    nb = G // C
    npair = RB_HEADS // 2
    W = RB_PROJ_W

    @pl.when(l == 0)
    def _init():
        z = jnp.zeros((bb, RB_N, RB_N), F32)
        for p in range(npair):
            top = jnp.concatenate([s0_ref[:, 2 * p], z], axis=2)
            bot = jnp.concatenate([z, s0_ref[:, 2 * p + 1]], axis=2)
            s2_s[:, p] = jnp.concatenate([top, bot], axis=1)

    rg, cg = _iota((G, G), 0), _iota((G, G), 1)
    same_chunk = (rg // C) == (cg // C)
    cum_m = (same_chunk & (rg >= cg)).astype(BF16)
    if nb > 1:
        cum_m = jnp.concatenate([cum_m, same_chunk.astype(BF16)], axis=0)
    rp, cp = _iota((G, 2 * RB_N), 0), _iota((G, 2 * RB_N), 1) % RB_N
    same_c = (rp // C) == (cp // C)
    tri_s = same_c & ((rp % C) > (cp % C))
    tri_i = same_c & ((rp % C) >= (cp % C))
    head0 = _iota((G, 2 * RB_N), 1) < RB_N
    same_head = (_iota((2 * RB_N, 2 * RB_N), 0) // RB_N) == (_iota((2 * RB_N, 2 * RB_N), 1) // RB_N)
    seg = ((_iota((MXU_W, MXU_W), 0) // RB_N) == (_iota((MXU_W, MXU_W), 1) // RB_N)).astype(BF16)
    eye2 = jnp.where(rp == cp, 1.0, 0.0)
    is_tanh = _iota((G, RB_LORA_WA), 1) < RB_LORA_WA // 2
    nsq = int(math.log2(C)) - 1

    def stack2(x):
        return jnp.concatenate([jnp.where(head0, x, 0.0), jnp.where(head0, 0.0, x)], axis=0)

    if tl >= G:
        n_iter, n_unit = tl // G, bb
    else:
        n_unit = min(UNITS_PER_ITER, bb * tl // G)
        n_iter = bb * tl // (G * n_unit)
    sls = [slice(p * 2 * RB_N, (p + 1) * 2 * RB_N) for p in range(npair)]

    def load_rows(it, u):
        if tl >= G:
            r0 = pl.multiple_of(it * G, G)
            x = pr_ref[u, pl.ds(r0, G), :]
            before = pr_ref[u, pl.ds(jnp.maximum(r0 - 1, 0), 1), :]
            head = jnp.where(l == 0, sh0_ref[u], prev_ref[u, 7:8, :])
            first = jnp.where(it == 0, head, before)
            prev = jnp.where(_iota((G, W), 0) == 0, first, pltpu.roll(x, 1, 0))
            return x, prev, [u], (u, pl.ds(r0, G))
        b0 = pl.multiple_of((it * n_unit + u) * nb, nb)
        x = pr_ref[pl.ds(b0, nb)]
        rolled = pltpu.roll(x.reshape(G, W), 1, 0).reshape(nb, tl, W)
        prev = jnp.where(_iota((nb, tl, W), 1) == 0, sh0_ref[pl.ds(b0, nb)], rolled)
        return x.reshape(G, W), prev.reshape(G, W), [b0 + j for j in range(nb)], (pl.ds(b0, nb),)

    def step(it, carry):
        units = []
        for u in range(n_unit):
            x, xprev, bs, out_idx = load_rows(it, u)
            pm = x + (xprev - x) * mu_ref[...]
            dwa = pm[:, 3 * RB_W:3 * RB_W + RB_LORA_WA]
            units.append(dict(bs=bs, out_idx=out_idx, r=pm[:, 0:RB_W], k=pm[:, RB_W:2 * RB_W],
                              v=pm[:, 2 * RB_W:3 * RB_W], dwa=jnp.where(is_tanh, jnp.tanh(dwa), dwa),
                              sdg=_sigmoid(pm[:, 3 * RB_W + RB_LORA_WA:])))
        wa = _dot(jnp.concatenate([q['dwa'] for q in units], axis=0), wwa_ref[...])
        gate = _dot(jnp.concatenate([q['sdg'] for q in units], axis=0), g2_ref[...])
        for i, q in enumerate(units):
            rs = slice(i * G, (i + 1) * G)
            q['lw'] = -RB_DECAY_SCALE * _sigmoid(w0_ref[...] + wa[rs, 0:RB_W])
            q['a'] = _sigmoid(a0_ref[...] + wa[rs, RB_W:])
            q['gate'] = gate[rs]
            q['kk'] = q['k'] * kk_ref[...]
            q['kp'] = q['k'] * (1.0 + (q['a'] - 1.0) * ka_ref[...])
        sums = _segsum(jnp.concatenate([q['kk'] * q['kk'] for q in units]
                                       + [q['r'] * q['kp'] * rk_ref[...] for q in units], axis=0), seg)
        tasks = []
        for i, q in enumerate(units):
            kk = q['kk'] * lax.rsqrt(jnp.maximum(sums[i * G:(i + 1) * G], 1e-24))
            q['bonus'] = sums[(n_unit + i) * G:(n_unit + i + 1) * G] * q['v']
            kb = kk * q['a']
            lw, kp = q['lw'], q['kp']
            gs = _sum3(cum_m, lw)
            g = gs[:G]
            gl = gs[G:] if nb > 1 else g[G - 1:G, :]
            eg, eng, egl = jnp.exp(g), jnp.exp(-g), jnp.exp(gl)
            At = -kk * (eg * jnp.exp(-lw))
            Bt = kb * eng
            Kt = kp * eng
            Rt = q['r'] * eg
            dec = egl * eng
            Ke = kp * dec
            Be = kb * dec
            q['tasks'] = []
            for p in range(npair):
                sl = sls[p]
                t = dict(p=p, bs=q['bs'], AR=jnp.concatenate([At[:, sl], Rt[:, sl]], axis=0), B=Bt[:, sl],
                         K=Kt[:, sl], V=q['v'][:, sl], Ke=Ke[:, sl], Be=Be[:, sl], egl=egl[:, sl])
                tasks.append(t)
                q['tasks'].append(t)
        for t in tasks:
            t['S2'] = [s2_s[b, t['p']] for b in t['bs']]
        for t in tasks:
            t['M'] = _dot_nt(t['AR'], jnp.concatenate([stack2(t['B']), stack2(t['K'])], axis=0))
        for t in tasks:
            if nb == 1:
                x0 = _dot_nt(t['AR'], t['S2'][0])
                t['x0a'], t['x0r'] = x0[:G], x0[G:]
            else:
                x0 = [_dot_nt(jnp.concatenate([t['AR'][j * C:(j + 1) * C], t['AR'][G + j * C:G + (j + 1) * C]],
                                              axis=0), t['S2'][j]) for j in range(nb)]
                t['x0a'] = jnp.concatenate([x[:C] for x in x0], axis=0)
                t['x0r'] = jnp.concatenate([x[C:] for x in x0], axis=0)
        for t in tasks:
            M = t['M']
            lak = jnp.where(tri_s, M[:G, 2 * RB_N:], 0.0)
            lrk = jnp.where(tri_i, M[G:, 2 * RB_N:], 0.0)
            lv = _dot(jnp.concatenate([lak, lrk], axis=0), stack2(t['V']))
            t['X'] = t['x0a'] + lv[:G]
            t['Yv'] = t['x0r'] + lv[G:]
            t['P'] = jnp.where(tri_s, M[:G, :2 * RB_N], 0.0)
        for t in tasks:
            t['T'] = eye2 + t['P']
        for k in range(nsq):
            for t in tasks:
                rhs = stack2(t['P'])
                if k == 0:
                    t['P'] = _dot(t['P'], rhs)
                else:
                    pt = _dot(jnp.concatenate([t['P'], t['T']], axis=0), rhs)
                    t['P'] = pt[:G]
                    t['T'] = t['T'] + pt[G:]
        for t in tasks:
            t['T'] = t['T'] + _dot(t['T'], stack2(t['P']))
        for t in tasks:
            t['SA'] = _dot(t['T'], stack2(t['X']))
        for t in tasks:
            lrb = jnp.where(tri_i, t['M'][G:, :2 * RB_N], 0.0)
            t['y'] = t['Yv'] + _dot(lrb, stack2(t['SA']))
        for t in tasks:
            for j in range(nb):
                js = slice(j * C, (j + 1) * C)
                upd = _dot_tn(jnp.concatenate([t['V'][js], t['SA'][js]], axis=0),
                              jnp.concatenate([t['Ke'][js], t['Be'][js]], axis=0))
                s2_s[t['bs'][j], t['p']] = t['S2'][j] * t['egl'][j * C:j * C + 1, :] + jnp.where(same_head, upd, 0.0)
        y = jnp.concatenate([jnp.concatenate([t['y'] for t in q['tasks']], axis=1) for q in units], axis=0)
        yc = y - _segsum(y, seg) * (1.0 / RB_N)
        var = _segsum(yc * yc, seg) * (1.0 / RB_N)
        yn = yc * lax.rsqrt(var + RB_GN_EPS) * gnw_ref[...] + gnb_ref[...]
        for i, q in enumerate(units):
            out = (yn[i * G:(i + 1) * G] + q['bonus']) * q['gate']
            if tl >= G:
                o_ref[q['out_idx']] = out
            else:
                o_ref[q['out_idx']] = out.reshape(nb, tl, RB_W)
        return carry

    lax.fori_loop(0, n_iter, step, 0)

    @pl.when(l == pl.num_programs(1) - 1)
    def _fin():
        sh_ref[...] = pr_ref[:, tl - 1:tl, :]
        for p in range(npair):
            S2 = s2_s[:, p]
            s_ref[:, 2 * p] = S2[:, :RB_N, :RB_N]
            s_ref[:, 2 * p + 1] = S2[:, RB_N:, RB_N:]


def _rwkv(pr, shift0, s0, mu, w0, a0, wwa, g2, k_k, k_a, r_k, gn_w, gn_b, bb, tl, C):
    B, L, _ = pr.shape
    const = lambda b, l: (0, 0)
    tok = lambda b, l: (b, l, 0)
    st = lambda b, l: (b, 0, 0, 0)
    vec = lambda x: x.reshape(1, -1)
    kern = functools.partial(_rwkv_kernel, bb=bb, tl=tl, C=C)
    return pl.pallas_call(
        kern,
        out_shape=(jax.ShapeDtypeStruct((B, L, RB_W), F32),
                   jax.ShapeDtypeStruct(s0.shape, F32),
                   jax.ShapeDtypeStruct((B, 1, RB_PROJ_W), F32)),
        grid=(B // bb, L // tl),
        in_specs=[pl.BlockSpec((bb, tl, RB_PROJ_W), tok),
                  pl.BlockSpec((bb, 8, RB_PROJ_W), lambda b, l: (b, jnp.maximum(l * (tl // 8) - 1, 0), 0)),
                  pl.BlockSpec((bb, 1, RB_PROJ_W), lambda b, l: (b, 0, 0)),
                  pl.BlockSpec((bb, RB_HEADS, RB_N, RB_N), st),
                  pl.BlockSpec((1, RB_PROJ_W), const),
                  pl.BlockSpec((1, RB_W), const),
                  pl.BlockSpec((1, RB_W), const),
                  pl.BlockSpec((RB_LORA_WA, 2 * RB_W), const),
                  pl.BlockSpec((RB_LORA_G, RB_W), const),
                  pl.BlockSpec((1, RB_W), const),
                  pl.BlockSpec((1, RB_W), const),
                  pl.BlockSpec((1, RB_W), const),
                  pl.BlockSpec((1, RB_W), const),
                  pl.BlockSpec((1, RB_W), const)],
        out_specs=(pl.BlockSpec((bb, tl, RB_W), tok),
                   pl.BlockSpec((bb, RB_HEADS, RB_N, RB_N), st),
                   pl.BlockSpec((bb, 1, RB_PROJ_W), lambda b, l: (b, 0, 0))),
        scratch_shapes=[pltpu.VMEM((bb, RB_HEADS // 2, 2 * RB_N, 2 * RB_N), F32)],
        compiler_params=pltpu.CompilerParams(dimension_semantics=("parallel", "arbitrary"),
                                             vmem_limit_bytes=VMEM_LIMIT_BYTES),
        name="rwkv",
    )(pr, pr, shift0.reshape(B, 1, RB_PROJ_W), s0, vec(mu), vec(w0), vec(a0), wwa, g2, vec(k_k), vec(k_a),
      vec(r_k), vec(gn_w), vec(gn_b))


def _mixffn_kernel(oa_ref, ob_ref, pg_ref, x_ref, mod_ref, nf_ref, nfin_ref, wua_ref, wub_ref, wo_ref, wfi_ref,
                   wfo_ref, y_ref):
    bb, tl, d = x_ref.shape

    def part(bsl, tsl):
        rows = lambda ref: ref[bsl, tsl, :]
        x = rows(x_ref)
        nb_, nt_ = x.shape[0], x.shape[1]
        R = nb_ * nt_
        mod = lambda i: mod_ref[bsl, i:i + 1, :]
        pg = rows(pg_ref).reshape(R, GATE_W)
        ua = _dot(rows(oa_ref).reshape(R, HA_W), wua_ref[...])
        ub = _dot(rows(ob_ref).reshape(R, RB_W), wub_ref[...])
        yield
        merged = _sigmoid(pg[:, :d]) * ua + _sigmoid(pg[:, d:]) * ub
        mix = _dot(merged, wo_ref[...]).reshape(nb_, nt_, d)
        yield
        x1 = x + mod(2) * mix
        xn = x1 * lax.rsqrt(jnp.mean(x1 * x1, axis=-1, keepdims=True) + RMS_EPS) * nf_ref[...]
        xf = (xn * (1.0 + mod(4)) + mod(3)).reshape(R, d)
        h = _dot(xf, wfi_ref[...])
        yield
        gate, up = h[:, :FFN_HIDDEN], h[:, FFN_HIDDEN:]
        act = gate * _sigmoid(gate) * up
        f = _dot(act, wfo_ref[...]).reshape(nb_, nt_, d)
        yield
        x2 = x1 + mod(5) * f
        y_ref[bsl, tsl, :] = x2 * lax.rsqrt(jnp.mean(x2 * x2, axis=-1, keepdims=True) + RMS_EPS) * nfin_ref[...]

    _run_staged(_row_parts(part, bb, tl, MIXFFN_SPLIT))


def _mixffn(oa, ob, pg, x, mod, mod_off, norm_ffn, norm_final, wua, wub, wo, wfi, wfo, bb, tl):
    B, L, d = x.shape
    mod_blk = mod_off // bb
    const = lambda b, l: (0, 0)
    tok = lambda b, l: (b, l, 0)
    wspec = lambda w: pl.BlockSpec(w.shape, const, pipeline_mode=pl.Buffered(1))
    return pl.pallas_call(
        _mixffn_kernel,
        out_shape=jax.ShapeDtypeStruct((B, L, d), F32),
        grid=(B // bb, L // tl),
        in_specs=[pl.BlockSpec((bb, tl, HA_W), tok),
                  pl.BlockSpec((bb, tl, RB_W), tok),
                  pl.BlockSpec((bb, tl, GATE_W), tok),
                  pl.BlockSpec((bb, tl, d), tok),
                  pl.BlockSpec((bb, 6, d), lambda b, l: (b + mod_blk, 0, 0)),
                  pl.BlockSpec((1, d), const),
                  pl.BlockSpec((1, d), const),
                  wspec(wua), wspec(wub), wspec(wo), wspec(wfi), wspec(wfo)],
        out_specs=pl.BlockSpec((bb, tl, d), tok),
        compiler_params=pltpu.CompilerParams(dimension_semantics=("parallel", "parallel"),
                                             vmem_limit_bytes=VMEM_LIMIT_BYTES),
        name="mixffn",
    )(oa, ob, pg, x, mod, norm_ffn.reshape(1, d), norm_final.reshape(1, d), wua, wub, wo, wfi, wfo)


def _block(x, mod, mod_off, s_h, s_r, shift, p, layer, tiles):
    assert mod_off % tiles['dense'][0] == 0
    ph, pr, pg = _inproj(x, mod, mod_off, p['norm_mix'], p['w_in'], *tiles['dense'])
    oa, s_h = _hgrn(ph, s_h, p['lower_bounds'], p['hgrn_norm'], layer, *tiles['hgrn'])
    ob, s_r, shift = _rwkv(pr, shift, s_r, p['mu'], p['w0'], p['a0'], p['wwa'], p['g2'], p['k_k'], p['k_a'], p['r_k'],
                    p['gn_w'], p['gn_b'], *tiles['rwkv'])
    y = _mixffn(oa, ob, pg, x, mod, mod_off, p['norm_ffn'], p['norm_final'], p['w_up_a'], p['w_up_b'], p['w_out'],
                p['w_ffn_in'], p['w_ffn_out'], *tiles['dense'])
    return y, s_h, s_r, shift.reshape(shift.shape[0], RB_PROJ_W)


PROMPT_TILES = {'dense': (1, 512), 'hgrn': (8, 128, 64, 8), 'rwkv': (4, 128, 64)}
SAMPLE_TILES = {'dense': (32, 8), 'hgrn': (16, 8, 8, 8), 'rwkv': (16, 8, 8)}


def kernel(x_prompt, x_sample, c_prompt, c_sample, state_hgrn, state_rwkv, state_shift, w_ada, b_ada, norm_mix, norm_ffn, w_in, hgrn_lower_bounds, hgrn_norm, rwkv_mu, rwkv_w0, rwkv_w2, rwkv_a0, rwkv_a2, rwkv_g2, rwkv_k_k, rwkv_k_a, rwkv_r_k, rwkv_gn_w, rwkv_gn_b, w_up_a, w_up_b, w_out, w_ffn_in, w_ffn_out, norm_final):
    depth = w_ada.shape[0]
    assert depth == 1, "the final norm is fused into the layer kernel"
    nbp, nbs = x_prompt.shape[0], x_sample.shape[0]

    hp = jnp.zeros((nbp, HA_HEADS, HA_DK, HA_DK), F32)
    rp = jnp.zeros((nbp, RB_HEADS, RB_N, RB_N), F32)
    sp = jnp.zeros((nbp, RB_PROJ_W), x_prompt.dtype)

    yp, ys = x_prompt, x_sample
    outs = [[] for _ in range(6)]
    for l in range(depth):
        half = RB_LORA_WA // 2
        zero = jnp.zeros((half, RB_W), F32)
        wwa = jnp.concatenate([jnp.concatenate([rwkv_w2[l], zero], axis=1),
                               jnp.concatenate([zero, rwkv_a2[l]], axis=1)], axis=0)
        p = {
            'norm_mix': norm_mix[l], 'norm_ffn': norm_ffn[l], 'norm_final': norm_final,
            'w_in': w_in[l].astype(BF16), 'lower_bounds': hgrn_lower_bounds, 'hgrn_norm': hgrn_norm[l],
            'mu': rwkv_mu[l], 'w0': rwkv_w0[l], 'a0': rwkv_a0[l], 'wwa': wwa.astype(BF16),
            'g2': rwkv_g2[l].astype(BF16), 'k_k': rwkv_k_k[l], 'k_a': rwkv_k_a[l], 'r_k': rwkv_r_k[l],
            'gn_w': rwkv_gn_w[l], 'gn_b': rwkv_gn_b[l], 'w_up_a': w_up_a[l].astype(BF16),
            'w_up_b': w_up_b[l].astype(BF16), 'w_out': w_out[l].astype(BF16),
            'w_ffn_in': w_ffn_in[l].astype(BF16), 'w_ffn_out': w_ffn_out[l].astype(BF16),
        }
        mod = _ada(jnp.concatenate([c_sample, c_prompt], axis=0), w_ada[l], b_ada[l])
        yp, h1, r1, s1 = _block(yp, mod, nbs, hp, rp, sp, p, l, PROMPT_TILES)
        ys, h2, r2, s2 = _block(ys, mod, 0, state_hgrn[l], state_rwkv[l], state_shift[l], p, l, SAMPLE_TILES)
        for lst, val in zip(outs, (h1, r1, s1, h2, r2, s2)):
            lst.append(val)
    stacked = [jnp.stack(o) for o in outs]
    return (yp, ys, *stacked)
```

```python
import functools
import math

import jax
import jax.numpy as jnp
from jax import lax
from jax.experimental import pallas as pl
from jax.experimental.pallas import tpu as pltpu

F32 = jnp.float32
BF16 = jnp.bfloat16

D_MODEL = 1024
HA_HEADS = 4
HA_DK = 128
HA_W = HA_HEADS * HA_DK
RB_HEADS = 8
RB_N = 64
RB_W = RB_HEADS * RB_N
RB_LORA_WA = 128
RB_LORA_G = 128
RB_PROJ_W = 3 * RB_W + RB_LORA_WA + RB_LORA_G
RB_DECAY_SCALE = math.exp(-0.5)
RB_GN_EPS = 64e-5
HG_W = 4 * HA_W
GATE_W = 2 * D_MODEL
PROJ_W = HG_W + RB_PROJ_W + GATE_W
FFN_HIDDEN = 2816
RMS_EPS = 1e-6
LOG2_E = math.log2(math.e)

MXU_W = 256
VMEM_LIMIT_BYTES = 56 * 1024 * 1024
UNITS_PER_ITER = 4
RWKV_GROUP = 64


def _dot(a, b):
    return jnp.dot(a.astype(BF16), b.astype(BF16), preferred_element_type=F32)


def _dot_nt(a, b):
    return lax.dot_general(a.astype(BF16), b.astype(BF16), (((1,), (1,)), ((), ())), preferred_element_type=F32)


def _dot_tn(a, b):
    return lax.dot_general(a.astype(BF16), b.astype(BF16), (((0,), (0,)), ((), ())), preferred_element_type=F32)


def _sigmoid(x):
    return 1.0 / (1.0 + jnp.exp(-x))


def _segsum(x, seg):
    w = seg.shape[0]
    xb = x.astype(BF16)
    return jnp.concatenate([jnp.dot(xb[:, i:i + w], seg, preferred_element_type=F32)
                            for i in range(0, x.shape[1], w)], axis=1)


def _sum3(m, x):
    hi = x.astype(BF16)
    r1 = x - hi.astype(F32)
    mid = r1.astype(BF16)
    lo = (r1 - mid.astype(F32)).astype(BF16)
    return (jnp.dot(m, hi, preferred_element_type=F32) + jnp.dot(m, mid, preferred_element_type=F32)
            + jnp.dot(m, lo, preferred_element_type=F32))


def _run_staged(gens):
    gens = list(gens)
    while gens:
        for g in list(gens):
            try:
                next(g)
            except StopIteration:
                gens.remove(g)


def _row_parts(part, bb, tl, ns):
    if bb == 1:
        step = tl // ns
        return [part(slice(None), slice(i * step, (i + 1) * step)) for i in range(ns)]
    step = bb // ns
    return [part(slice(i * step, (i + 1) * step), slice(None)) for i in range(ns)]


def _iota(shape, axis):
    return lax.broadcasted_iota(jnp.int32, shape, axis)


def _ada_kernel(c_ref, w_ref, b_ref, o_ref):
    c = c_ref[...]
    val = _dot(c * _sigmoid(c), w_ref[...]) + b_ref[...]
    for j in range(o_ref.shape[1]):
        @pl.when(pl.program_id(0) == j)
        def _store():
            o_ref[:, j, :] = val


def _ada(c_all, w_ada, b_ada):
    n, d = c_all.shape
    nmod = w_ada.shape[1] // d
    return pl.pallas_call(
        _ada_kernel,
        out_shape=jax.ShapeDtypeStruct((n, nmod, d), F32),
        grid=(nmod,),
        in_specs=[pl.BlockSpec((n, d), lambda j: (0, 0)),
                  pl.BlockSpec((d, d), lambda j: (0, j)),
                  pl.BlockSpec((1, d), lambda j: (0, j))],
        out_specs=pl.BlockSpec((n, nmod, d), lambda j: (0, 0, 0)),
        compiler_params=pltpu.CompilerParams(dimension_semantics=("arbitrary",)),
        name="ada",
    )(c_all, w_ada, b_ada.reshape(1, nmod * d))


def _inproj_kernel(x_ref, mod_ref, g_ref, w_ref, oh_ref, or_ref, og_ref, *, ns):
    bb, tl, d = x_ref.shape

    def part(bsl, tsl):
        x = x_ref[bsl, tsl, :]
        nb_, nt_ = x.shape[0], x.shape[1]
        xn = x * lax.rsqrt(jnp.mean(x * x, axis=-1, keepdims=True) + RMS_EPS) * g_ref[...]
        xm = xn * (1.0 + mod_ref[bsl, 1:2, :]) + mod_ref[bsl, 0:1, :]
        xb = xm.reshape(nb_ * nt_, d).astype(BF16)
        yield
        lo = 0
        for o_ref in (oh_ref, or_ref, og_ref):
            w = o_ref.shape[-1]
            o_ref[bsl, tsl, :] = jnp.dot(xb, w_ref[:, lo:lo + w], preferred_element_type=F32).reshape(nb_, nt_, w)
            lo += w
            yield

    _run_staged(_row_parts(part, bb, tl, ns))


def _inproj(x, mod, mod_off, norm_g, w_in_bf16, bb, tl, ns):
    B, L, d = x.shape
    mod_blk = mod_off // bb
    const = lambda b, l: (0, 0)
    tok = lambda b, l: (b, l, 0)
    return pl.pallas_call(
        functools.partial(_inproj_kernel, ns=ns),
        out_shape=(jax.ShapeDtypeStruct((B, L, HG_W), F32),
                   jax.ShapeDtypeStruct((B, L, RB_PROJ_W), F32),
                   jax.ShapeDtypeStruct((B, L, GATE_W), F32)),
        grid=(B // bb, L // tl),
        in_specs=[pl.BlockSpec((bb, tl, d), tok),
                  pl.BlockSpec((bb, 6, d), lambda b, l: (b + mod_blk, 0, 0)),
                  pl.BlockSpec((1, d), const),
                  pl.BlockSpec((d, PROJ_W), const, pipeline_mode=pl.Buffered(1))],
        out_specs=(pl.BlockSpec((bb, tl, HG_W), tok),
                   pl.BlockSpec((bb, tl, RB_PROJ_W), tok),
                   pl.BlockSpec((bb, tl, GATE_W), tok)),
        compiler_params=pltpu.CompilerParams(dimension_semantics=("parallel", "parallel"),
                                             vmem_limit_bytes=VMEM_LIMIT_BYTES),
        name="inproj",
    )(x, mod, norm_g.reshape(1, d), w_in_bf16)


def _hgrn_kernel(ph_ref, s0_ref, lbp_ref, ng_ref, o_ref, s_ref, c2_s, *, layer, bb, nch, C, c):
    @pl.when(pl.program_id(1) == 0)
    def _init():
        s_ref[...] = s0_ref[...]

    hb = lbp_ref[...]
    e = jnp.exp(hb - jnp.max(hb, axis=0, keepdims=True))
    lb = jnp.sum(e[0:layer + 1], axis=0, keepdims=True) / jnp.sum(e, axis=0, keepdims=True)
    ng = ng_ref[...]
    cum_m = (_iota((C, C), 0) >= _iota((C, C), 1)).astype(BF16)
    nsub = C // c
    causal = _iota((C, C), 0) >= _iota((C, C), 1)

    def one_chunk(ci):
        r0 = pl.multiple_of(ci * C, C)
        tasks, units = [], []
        for u in range(bb):
            ph = ph_ref[u, pl.ds(r0, C), :]
            hq = ph[:, 0:HA_W]
            hf = ph[:, HA_W:2 * HA_W]
            hi = ph[:, 2 * HA_W:3 * HA_W]
            hg = ph[:, 3 * HA_W:4 * HA_W]
            q = hq * _sigmoid(hq)
            sg = _sigmoid(hf)
            logf = jnp.log(lb + (1.0 - lb) * sg)
            b2 = _sum3(cum_m, logf) * LOG2_E
            c2 = b2 - jnp.log2((1.0 - lb) * (1.0 - sg))
            c2_s[u] = c2
            unit = dict(u=u, gate=hg * _sigmoid(hg), tasks=[])
            for h in range(HA_HEADS):
                sl = slice(h * HA_DK, (h + 1) * HA_DK)
                t = dict(u=u, h=h, q=q[:, sl], v=hi[:, sl], b2=b2[:, sl], c2=c2[:, sl])
                tasks.append(t)
                unit['tasks'].append(t)
            units.append(unit)
        for t in tasks:
            t['S'] = s_ref[t['u'], t['h']]
        for t in tasks:
            t['o'] = _dot(t['q'] * jnp.exp2(t['b2']), t['S'])
        lane_is = [_iota((c, C), 1) == j for j in range(C)]
        for t in tasks:
            q, b2, c2 = t['q'], t['b2'], t['c2']
            sl = slice(t['h'] * HA_DK, (t['h'] + 1) * HA_DK)
            rows = []
            for i in range(nsub):
                lo = i * c
                qi, bi = q[lo:lo + c], b2[lo:lo + c]
                if i > 0:
                    bref = b2[lo - 1:lo, :]
                    kt = jnp.concatenate([jnp.exp2(bref - c2[:lo]), jnp.zeros((C - lo, HA_DK), F32)], axis=0)
                    sc = _dot_nt(qi * jnp.exp2(bi - bref), kt)
                else:
                    sc = jnp.zeros((c, C), F32)
                for s in range(c):
                    col = jnp.sum(qi * jnp.exp2(bi - c2_s[t['u'], lo + s:lo + s + 1, sl]), axis=-1, keepdims=True)
                    sc = jnp.where(lane_is[lo + s], col, sc)
                rows.append(sc)
            t['scores'] = jnp.where(causal, jnp.concatenate(rows, axis=0), 0.0)
        for t in tasks:
            t['o'] = t['o'] + _dot(t['scores'], t['v'])
        for t in tasks:
            bl = t['b2'][C - 1:C, :]
            kd = jnp.exp2(bl - t['c2'])
            dcol = jnp.transpose(jnp.broadcast_to(jnp.exp2(bl), (8, HA_DK)))[:, 0:1]
            s_ref[t['u'], t['h']] = t['S'] * dcol + _dot_tn(kd, t['v'])
        for unit in units:
            outs = [t['o'] * lax.rsqrt(jnp.mean(t['o'] * t['o'], axis=-1, keepdims=True) + RMS_EPS)
                    for t in unit['tasks']]
            o_ref[unit['u'], pl.ds(r0, C), :] = jnp.concatenate(outs, axis=1) * ng * unit['gate']

    def per_chunk(ci, carry):
        one_chunk(ci)
        return carry

    lax.fori_loop(0, nch, per_chunk, 0)


def _hgrn(ph, s0, lower_bounds, norm_g, layer, bb, tl, C, c):
    B, L, _ = ph.shape
    const = lambda b, l: (0, 0)
    st = lambda b, l: (b, 0, 0, 0)
    kern = functools.partial(_hgrn_kernel, layer=layer, bb=bb, nch=tl // C, C=C, c=c)
    return pl.pallas_call(
        kern,
        out_shape=(jax.ShapeDtypeStruct((B, L, HA_W), F32),
                   jax.ShapeDtypeStruct(s0.shape, F32)),
        grid=(B // bb, L // tl),
        in_specs=[pl.BlockSpec((bb, tl, HG_W), lambda b, l: (b, l, 0)),
                  pl.BlockSpec((bb, HA_HEADS, HA_DK, HA_DK), st),
                  pl.BlockSpec(lower_bounds.shape, const),
                  pl.BlockSpec((1, HA_W), const)],
        out_specs=(pl.BlockSpec((bb, tl, HA_W), lambda b, l: (b, l, 0)),
                   pl.BlockSpec((bb, HA_HEADS, HA_DK, HA_DK), st)),
        scratch_shapes=[pltpu.VMEM((bb, C, HA_W), F32)],
        compiler_params=pltpu.CompilerParams(dimension_semantics=("parallel", "arbitrary"),
                                             vmem_limit_bytes=VMEM_LIMIT_BYTES),
        name="hgrn",
    )(ph, s0, lower_bounds, norm_g.reshape(1, HA_W))


def _rwkv_kernel(pr_ref, prev_ref, sh0_ref, s0_ref, mu_ref, w0_ref, a0_ref, wwa_ref, g2_ref, kk_ref, ka_ref,
                 rk_ref, gnw_ref, gnb_ref, o_ref, s_ref, sh_ref, s2_s, *, bb, tl, C):
    l = pl.program_id(1)
    G = RWKV_GROUP
    nb = G // C
    npair = RB_HEADS // 2
    W = RB_PROJ_W

    @pl.when(l == 0)
    def _init():
        z = jnp.zeros((bb, RB_N, RB_N), F32)
        for p in range(npair):
            top = jnp.concatenate([s0_ref[:, 2 * p], z], axis=2)
            bot = jnp.concatenate([z, s0_ref[:, 2 * p + 1]], axis=2)
            s2_s[:, p] = jnp.concatenate([top, bot], axis=1)

    rg, cg = _iota((G, G), 0), _iota((G, G), 1)
    same_chunk = (rg // C) == (cg // C)
    cum_m = (same_chunk & (rg >= cg)).astype(BF16)
    if nb > 1:
        cum_m = jnp.concatenate([cum_m, same_chunk.astype(BF16)], axis=0)
    rp, cp = _iota((G, 2 * RB_N), 0), _iota((G, 2 * RB_N), 1) % RB_N
    same_c = (rp // C) == (cp // C)
    tri_s = same_c & ((rp % C) > (cp % C))
    tri_i = same_c & ((rp % C) >= (cp % C))
    head0 = _iota((G, 2 * RB_N), 1) < RB_N
    same_head = (_iota((2 * RB_N, 2 * RB_N), 0) // RB_N) == (_iota((2 * RB_N, 2 * RB_N), 1) // RB_N)
    seg = ((_iota((MXU_W, MXU_W), 0) // RB_N) == (_iota((MXU_W, MXU_W), 1) // RB_N)).astype(BF16)
    eye2 = jnp.where(rp == cp, 1.0, 0.0)
    is_tanh = _iota((G, RB_LORA_WA), 1) < RB_LORA_WA // 2
    nsq = int(math.log2(C)) - 1

    def stack2(x):
        return jnp.concatenate([jnp.where(head0, x, 0.0), jnp.where(head0, 0.0, x)], axis=0)

    if tl >= G:
        n_iter, n_unit = tl // G, bb
    else:
        n_unit = min(UNITS_PER_ITER, bb * tl // G)
        n_iter = bb * tl // (G * n_unit)
    sls = [slice(p * 2 * RB_N, (p + 1) * 2 * RB_N) for p in range(npair)]

    def load_rows(it, u):
        if tl >= G:
            r0 = pl.multiple_of(it * G, G)
            x = pr_ref[u, pl.ds(r0, G), :]
            before = pr_ref[u, pl.ds(jnp.maximum(r0 - 1, 0), 1), :]
            head = jnp.where(l == 0, sh0_ref[u], prev_ref[u, 7:8, :])
            first = jnp.where(it == 0, head, before)
            prev = jnp.where(_iota((G, W), 0) == 0, first, pltpu.roll(x, 1, 0))
            return x, prev, [u], (u, pl.ds(r0, G))
        b0 = pl.multiple_of((it * n_unit + u) * nb, nb)
        x = pr_ref[pl.ds(b0, nb)]
        rolled = pltpu.roll(x.reshape(G, W), 1, 0).reshape(nb, tl, W)
        prev = jnp.where(_iota((nb, tl, W), 1) == 0, sh0_ref[pl.ds(b0, nb)], rolled)
        return x.reshape(G, W), prev.reshape(G, W), [b0 + j for j in range(nb)], (pl.ds(b0, nb),)

    def step(it, carry):
        units = []
        for u in range(n_unit):
            x, xprev, bs, out_idx = load_rows(it, u)
            pm = x + (xprev - x) * mu_ref[...]
            dwa = pm[:, 3 * RB_W:3 * RB_W + RB_LORA_WA]
            units.append(dict(bs=bs, out_idx=out_idx, r=pm[:, 0:RB_W], k=pm[:, RB_W:2 * RB_W],
                              v=pm[:, 2 * RB_W:3 * RB_W], dwa=jnp.where(is_tanh, jnp.tanh(dwa), dwa),
                              sdg=_sigmoid(pm[:, 3 * RB_W + RB_LORA_WA:])))
        wa = _dot(jnp.concatenate([q['dwa'] for q in units], axis=0), wwa_ref[...])
        gate = _dot(jnp.concatenate([q['sdg'] for q in units], axis=0), g2_ref[...])
        for i, q in enumerate(units):
            rs = slice(i * G, (i + 1) * G)
            q['lw'] = -RB_DECAY_SCALE * _sigmoid(w0_ref[...] + wa[rs, 0:RB_W])
            q['a'] = _sigmoid(a0_ref[...] + wa[rs, RB_W:])
            q['gate'] = gate[rs]
            q['kk'] = q['k'] * kk_ref[...]
            q['kp'] = q['k'] * (1.0 + (q['a'] - 1.0) * ka_ref[...])
        sums = _segsum(jnp.concatenate([q['kk'] * q['kk'] for q in units]
                                       + [q['r'] * q['kp'] * rk_ref[...] for q in units], axis=0), seg)
        tasks = []
        for i, q in enumerate(units):
            kk = q['kk'] * lax.rsqrt(jnp.maximum(sums[i * G:(i + 1) * G], 1e-24))
            q['bonus'] = sums[(n_unit + i) * G:(n_unit + i + 1) * G] * q['v']
            kb = kk * q['a']
            lw, kp = q['lw'], q['kp']
            gs = _sum3(cum_m, lw)
            g = gs[:G]
            gl = gs[G:] if nb > 1 else g[G - 1:G, :]
            eg, eng, egl = jnp.exp(g), jnp.exp(-g), jnp.exp(gl)
            At = -kk * (eg * jnp.exp(-lw))
            Bt = kb * eng
            Kt = kp * eng
            Rt = q['r'] * eg
            dec = egl * eng
            Ke = kp * dec
            Be = kb * dec
            q['tasks'] = []
            for p in range(npair):
                sl = sls[p]
                t = dict(p=p, bs=q['bs'], AR=jnp.concatenate([At[:, sl], Rt[:, sl]], axis=0), B=Bt[:, sl],
                         K=Kt[:, sl], V=q['v'][:, sl], Ke=Ke[:, sl], Be=Be[:, sl], egl=egl[:, sl])
                tasks.append(t)
                q['tasks'].append(t)
        for t in tasks:
            t['S2'] = [s2_s[b, t['p']] for b in t['bs']]
        for t in tasks:
            t['M'] = _dot_nt(t['AR'], jnp.concatenate([stack2(t['B']), stack2(t['K'])], axis=0))
        for t in tasks:
            if nb == 1:
                x0 = _dot_nt(t['AR'], t['S2'][0])
                t['x0a'], t['x0r'] = x0[:G], x0[G:]
            else:
                x0 = [_dot_nt(jnp.concatenate([t['AR'][j * C:(j + 1) * C], t['AR'][G + j * C:G + (j + 1) * C]],
                                              axis=0), t['S2'][j]) for j in range(nb)]
                t['x0a'] = jnp.concatenate([x[:C] for x in x0], axis=0)
                t['x0r'] = jnp.concatenate([x[C:] for x in x0], axis=0)
        for t in tasks:
            M = t['M']
            lak = jnp.where(tri_s, M[:G, 2 * RB_N:], 0.0)
            lrk = jnp.where(tri_i, M[G:, 2 * RB_N:], 0.0)
            lv = _dot(jnp.concatenate([lak, lrk], axis=0), stack2(t['V']))
            t['X'] = t['x0a'] + lv[:G]
            t['Yv'] = t['x0r'] + lv[G:]
            t['P'] = jnp.where(tri_s, M[:G, :2 * RB_N], 0.0)
        for t in tasks:
            t['T'] = eye2 + t['P']
        for k in range(nsq):
            for t in tasks:
                rhs = stack2(t['P'])
                if k == 0:
                    t['P'] = _dot(t['P'], rhs)
                else:
                    pt = _dot(jnp.concatenate([t['P'], t['T']], axis=0), rhs)
                    t['P'] = pt[:G]
                    t['T'] = t['T'] + pt[G:]
        for t in tasks:
            t['T'] = t['T'] + _dot(t['T'], stack2(t['P']))
        for t in tasks:
            t['SA'] = _dot(t['T'], stack2(t['X']))
        for t in tasks:
            lrb = jnp.where(tri_i, t['M'][G:, :2 * RB_N], 0.0)
            t['y'] = t['Yv'] + _dot(lrb, stack2(t['SA']))
        for t in tasks:
            for j in range(nb):
                js = slice(j * C, (j + 1) * C)
                upd = _dot_tn(jnp.concatenate([t['V'][js], t['SA'][js]], axis=0),
                              jnp.concatenate([t['Ke'][js], t['Be'][js]], axis=0))
                s2_s[t['bs'][j], t['p']] = t['S2'][j] * t['egl'][j * C:j * C + 1, :] + jnp.where(same_head, upd, 0.0)
        y = jnp.concatenate([jnp.concatenate([t['y'] for t in q['tasks']], axis=1) for q in units], axis=0)
        yc = y - _segsum(y, seg) * (1.0 / RB_N)
        var = _segsum(yc * yc, seg) * (1.0 / RB_N)
        yn = yc * lax.rsqrt(var + RB_GN_EPS) * gnw_ref[...] + gnb_ref[...]
        for i, q in enumerate(units):
            out = (yn[i * G:(i + 1) * G] + q['bonus']) * q['gate']
            if tl >= G:
                o_ref[q['out_idx']] = out
            else:
                o_ref[q['out_idx']] = out.reshape(nb, tl, RB_W)
        return carry

    lax.fori_loop(0, n_iter, step, 0)

    @pl.when(l == pl.num_programs(1) - 1)
    def _fin():
        sh_ref[...] = pr_ref[:, tl - 1:tl, :]
        for p in range(npair):
            S2 = s2_s[:, p]
            s_ref[:, 2 * p] = S2[:, :RB_N, :RB_N]
            s_ref[:, 2 * p + 1] = S2[:, RB_N:, RB_N:]


def _rwkv(pr, shift0, s0, mu, w0, a0, wwa, g2, k_k, k_a, r_k, gn_w, gn_b, bb, tl, C):
    B, L, _ = pr.shape
    const = lambda b, l: (0, 0)
    tok = lambda b, l: (b, l, 0)
    st = lambda b, l: (b, 0, 0, 0)
    vec = lambda x: x.reshape(1, -1)
    kern = functools.partial(_rwkv_kernel, bb=bb, tl=tl, C=C)
    return pl.pallas_call(
        kern,
        out_shape=(jax.ShapeDtypeStruct((B, L, RB_W), F32),
                   jax.ShapeDtypeStruct(s0.shape, F32),
                   jax.ShapeDtypeStruct((B, 1, RB_PROJ_W), F32)),
        grid=(B // bb, L // tl),
        in_specs=[pl.BlockSpec((bb, tl, RB_PROJ_W), tok),
                  pl.BlockSpec((bb, 8, RB_PROJ_W), lambda b, l: (b, jnp.maximum(l * (tl // 8) - 1, 0), 0)),
                  pl.BlockSpec((bb, 1, RB_PROJ_W), lambda b, l: (b, 0, 0)),
                  pl.BlockSpec((bb, RB_HEADS, RB_N, RB_N), st),
                  pl.BlockSpec((1, RB_PROJ_W), const),
                  pl.BlockSpec((1, RB_W), const),
                  pl.BlockSpec((1, RB_W), const),
                  pl.BlockSpec((RB_LORA_WA, 2 * RB_W), const),
                  pl.BlockSpec((RB_LORA_G, RB_W), const),
                  pl.BlockSpec((1, RB_W), const),
                  pl.BlockSpec((1, RB_W), const),
                  pl.BlockSpec((1, RB_W), const),
                  pl.BlockSpec((1, RB_W), const),
                  pl.BlockSpec((1, RB_W), const)],
        out_specs=(pl.BlockSpec((bb, tl, RB_W), tok),
                   pl.BlockSpec((bb, RB_HEADS, RB_N, RB_N), st),
                   pl.BlockSpec((bb, 1, RB_PROJ_W), lambda b, l: (b, 0, 0))),
        scratch_shapes=[pltpu.VMEM((bb, RB_HEADS // 2, 2 * RB_N, 2 * RB_N), F32)],
        compiler_params=pltpu.CompilerParams(dimension_semantics=("parallel", "arbitrary"),
                                             vmem_limit_bytes=VMEM_LIMIT_BYTES),
        name="rwkv",
    )(pr, pr, shift0.reshape(B, 1, RB_PROJ_W), s0, vec(mu), vec(w0), vec(a0), wwa, g2, vec(k_k), vec(k_a),
      vec(r_k), vec(gn_w), vec(gn_b))


def _mixffn_kernel(oa_ref, ob_ref, pg_ref, x_ref, mod_ref, nf_ref, nfin_ref, wua_ref, wub_ref, wo_ref, wfi_ref,
                   wfo_ref, y_ref, *, ns):
    bb, tl, d = x_ref.shape

    def part(bsl, tsl):
        rows = lambda ref: ref[bsl, tsl, :]
        x = rows(x_ref)
        nb_, nt_ = x.shape[0], x.shape[1]
        R = nb_ * nt_
        mod = lambda i: mod_ref[bsl, i:i + 1, :]
        pg = rows(pg_ref).reshape(R, GATE_W)
        ua = _dot(rows(oa_ref).reshape(R, HA_W), wua_ref[...])
        ub = _dot(rows(ob_ref).reshape(R, RB_W), wub_ref[...])
        yield
        merged = _sigmoid(pg[:, :d]) * ua + _sigmoid(pg[:, d:]) * ub
        mix = _dot(merged, wo_ref[...]).reshape(nb_, nt_, d)
        yield
        x1 = x + mod(2) * mix
        xn = x1 * lax.rsqrt(jnp.mean(x1 * x1, axis=-1, keepdims=True) + RMS_EPS) * nf_ref[...]
        xf = (xn * (1.0 + mod(4)) + mod(3)).reshape(R, d)
        h = _dot(xf, wfi_ref[...])
        yield
        gate, up = h[:, :FFN_HIDDEN], h[:, FFN_HIDDEN:]
        act = gate * _sigmoid(gate) * up
        f = _dot(act, wfo_ref[...]).reshape(nb_, nt_, d)
        yield
        x2 = x1 + mod(5) * f
        y_ref[bsl, tsl, :] = x2 * lax.rsqrt(jnp.mean(x2 * x2, axis=-1, keepdims=True) + RMS_EPS) * nfin_ref[...]

    _run_staged(_row_parts(part, bb, tl, ns))


def _mixffn(oa, ob, pg, x, mod, mod_off, norm_ffn, norm_final, wua, wub, wo, wfi, wfo, bb, tl, ns):
    B, L, d = x.shape
    mod_blk = mod_off // bb
    const = lambda b, l: (0, 0)
    tok = lambda b, l: (b, l, 0)
    wspec = lambda w: pl.BlockSpec(w.shape, const, pipeline_mode=pl.Buffered(1))
    return pl.pallas_call(
        functools.partial(_mixffn_kernel, ns=ns),
        out_shape=jax.ShapeDtypeStruct((B, L, d), F32),
        grid=(B // bb, L // tl),
        in_specs=[pl.BlockSpec((bb, tl, HA_W), tok),
                  pl.BlockSpec((bb, tl, RB_W), tok),
                  pl.BlockSpec((bb, tl, GATE_W), tok),
                  pl.BlockSpec((bb, tl, d), tok),
                  pl.BlockSpec((bb, 6, d), lambda b, l: (b + mod_blk, 0, 0)),
                  pl.BlockSpec((1, d), const),
                  pl.BlockSpec((1, d), const),
                  wspec(wua), wspec(wub), wspec(wo), wspec(wfi), wspec(wfo)],
        out_specs=pl.BlockSpec((bb, tl, d), tok),
        compiler_params=pltpu.CompilerParams(dimension_semantics=("parallel", "parallel"),
                                             vmem_limit_bytes=VMEM_LIMIT_BYTES),
        name="mixffn",
    )(oa, ob, pg, x, mod, norm_ffn.reshape(1, d), norm_final.reshape(1, d), wua, wub, wo, wfi, wfo)


def _block(x, mod, mod_off, s_h, s_r, shift, p, layer, tiles):
    assert mod_off % tiles['inproj'][0] == 0 and mod_off % tiles['mixffn'][0] == 0
    ph, pr, pg = _inproj(x, mod, mod_off, p['norm_mix'], p['w_in'], *tiles['inproj'])
    oa, s_h = _hgrn(ph, s_h, p['lower_bounds'], p['hgrn_norm'], layer, *tiles['hgrn'])
    ob, s_r, shift = _rwkv(pr, shift, s_r, p['mu'], p['w0'], p['a0'], p['wwa'], p['g2'], p['k_k'], p['k_a'], p['r_k'],
                    p['gn_w'], p['gn_b'], *tiles['rwkv'])
    y = _mixffn(oa, ob, pg, x, mod, mod_off, p['norm_ffn'], p['norm_final'], p['w_up_a'], p['w_up_b'], p['w_out'],
                p['w_ffn_in'], p['w_ffn_out'], *tiles['mixffn'])
    return y, s_h, s_r, shift.reshape(shift.shape[0], RB_PROJ_W)


PROMPT_TILES = {'inproj': (1, 512, 4), 'mixffn': (1, 512, 2), 'hgrn': (8, 128, 64, 8), 'rwkv': (4, 128, 64)}
SAMPLE_TILES = {'inproj': (64, 8, 2), 'mixffn': (64, 8, 2), 'hgrn': (16, 8, 8, 8), 'rwkv': (32, 8, 8)}


def kernel(x_prompt, x_sample, c_prompt, c_sample, state_hgrn, state_rwkv, state_shift, w_ada, b_ada, norm_mix, norm_ffn, w_in, hgrn_lower_bounds, hgrn_norm, rwkv_mu, rwkv_w0, rwkv_w2, rwkv_a0, rwkv_a2, rwkv_g2, rwkv_k_k, rwkv_k_a, rwkv_r_k, rwkv_gn_w, rwkv_gn_b, w_up_a, w_up_b, w_out, w_ffn_in, w_ffn_out, norm_final):
    depth = w_ada.shape[0]
    assert depth == 1, "the final norm is fused into the layer kernel"
    nbp, nbs = x_prompt.shape[0], x_sample.shape[0]

    hp = jnp.zeros((nbp, HA_HEADS, HA_DK, HA_DK), F32)
    rp = jnp.zeros((nbp, RB_HEADS, RB_N, RB_N), F32)
    sp = jnp.zeros((nbp, RB_PROJ_W), x_prompt.dtype)

    yp, ys = x_prompt, x_sample
    outs = [[] for _ in range(6)]
    for l in range(depth):
        half = RB_LORA_WA // 2
        zero = jnp.zeros((half, RB_W), F32)
        wwa = jnp.concatenate([jnp.concatenate([rwkv_w2[l], zero], axis=1),
                               jnp.concatenate([zero, rwkv_a2[l]], axis=1)], axis=0)
        p = {
            'norm_mix': norm_mix[l], 'norm_ffn': norm_ffn[l], 'norm_final': norm_final,
            'w_in': w_in[l].astype(BF16), 'lower_bounds': hgrn_lower_bounds, 'hgrn_norm': hgrn_norm[l],
            'mu': rwkv_mu[l], 'w0': rwkv_w0[l], 'a0': rwkv_a0[l], 'wwa': wwa.astype(BF16),
            'g2': rwkv_g2[l].astype(BF16), 'k_k': rwkv_k_k[l], 'k_a': rwkv_k_a[l], 'r_k': rwkv_r_k[l],
            'gn_w': rwkv_gn_w[l], 'gn_b': rwkv_gn_b[l], 'w_up_a': w_up_a[l].astype(BF16),
            'w_up_b': w_up_b[l].astype(BF16), 'w_out': w_out[l].astype(BF16),
            'w_ffn_in': w_ffn_in[l].astype(BF16), 'w_ffn_out': w_ffn_out[l].astype(BF16),
        }
        mod = _ada(jnp.concatenate([c_sample, c_prompt], axis=0), w_ada[l], b_ada[l])
        yp, h1, r1, s1 = _block(yp, mod, nbs, hp, rp, sp, p, l, PROMPT_TILES)
        ys, h2, r2, s2 = _block(ys, mod, 0, state_hgrn[l], state_rwkv[l], state_shift[l], p, l, SAMPLE_TILES)
        for lst, val in zip(outs, (h1, r1, s1, h2, r2, s2)):
            lst.append(val)
    stacked = [jnp.stack(o) for o in outs]
    return (yp, ys, *stacked)
```

```python
import functools
import math

import jax
import jax.numpy as jnp
from jax import lax
from jax.experimental import pallas as pl
from jax.experimental.pallas import tpu as pltpu

F32 = jnp.float32
BF16 = jnp.bfloat16

D_MODEL = 1024
HA_HEADS = 4
HA_DK = 128
HA_W = HA_HEADS * HA_DK
RB_HEADS = 8
RB_N = 64
RB_W = RB_HEADS * RB_N
RB_LORA_WA = 128
RB_LORA_G = 128
RB_PROJ_W = 3 * RB_W + RB_LORA_WA + RB_LORA_G
RB_DECAY_SCALE = math.exp(-0.5)
RB_GN_EPS = 64e-5
HG_W = 4 * HA_W
GATE_W = 2 * D_MODEL
PROJ_W = HG_W + RB_PROJ_W + GATE_W
FFN_HIDDEN = 2816
RMS_EPS = 1e-6
LOG2_E = math.log2(math.e)

MXU_W = 256
VMEM_LIMIT_BYTES = 58 * 1024 * 1024
STAGE_SHAPE = (1024, 512)
UNITS_PER_ITER = 4
RWKV_GROUP = 64


def _dot(a, b):
    return jnp.dot(a.astype(BF16), b.astype(BF16), preferred_element_type=F32)


def _dot_nt(a, b):
    return lax.dot_general(a.astype(BF16), b.astype(BF16), (((1,), (1,)), ((), ())), preferred_element_type=F32)


def _dot_tn(a, b):
    return lax.dot_general(a.astype(BF16), b.astype(BF16), (((0,), (0,)), ((), ())), preferred_element_type=F32)


def _sigmoid(x):
    return 1.0 / (1.0 + jnp.exp(-x))


def _segsum(x, seg):
    w = seg.shape[0]
    xb = x.astype(BF16)
    return jnp.concatenate([jnp.dot(xb[:, i:i + w], seg, preferred_element_type=F32)
                            for i in range(0, x.shape[1], w)], axis=1)


def _sum3(m, x):
    hi = x.astype(BF16)
    r1 = x - hi.astype(F32)
    mid = r1.astype(BF16)
    lo = (r1 - mid.astype(F32)).astype(BF16)
    return (jnp.dot(m, hi, preferred_element_type=F32) + jnp.dot(m, mid, preferred_element_type=F32)
            + jnp.dot(m, lo, preferred_element_type=F32))


def _run_staged(gens):
    gens = list(gens)
    while gens:
        for g in list(gens):
            try:
                next(g)
            except StopIteration:
                gens.remove(g)


def _row_parts(part, bb, tl, ns):
    if bb == 1:
        step = tl // ns
        return [part(slice(None), slice(i * step, (i + 1) * step)) for i in range(ns)]
    step = bb // ns
    return [part(slice(i * step, (i + 1) * step), slice(None)) for i in range(ns)]


def _cast_weights(pairs, stage, sem):
    srows, scols = stage.shape[1], stage.shape[2]
    chunks = []
    for src, dst in pairs:
        nrow, ncol = src.shape
        for r0 in range(0, nrow, srows):
            for c0 in range(0, ncol, scols):
                chunks.append((src, dst, r0, min(srows, nrow - r0), c0, min(scols, ncol - c0)))

    def copy(i):
        src, _, r0, nr, c0, nc = chunks[i]
        return pltpu.make_async_copy(src.at[pl.ds(r0, nr), pl.ds(c0, nc)],
                                     stage.at[i % 2, pl.ds(0, nr), pl.ds(0, nc)], sem.at[i % 2])

    copy(0).start()
    for i, (_, dst, r0, nr, c0, nc) in enumerate(chunks):
        if i + 1 < len(chunks):
            copy(i + 1).start()
        copy(i).wait()
        dst[r0:r0 + nr, c0:c0 + nc] = stage[i % 2, 0:nr, 0:nc].astype(BF16)


def _iota(shape, axis):
    return lax.broadcasted_iota(jnp.int32, shape, axis)


def _ada_kernel(c_ref, w_ref, b_ref, o_ref):
    c = c_ref[...]
    val = _dot(c * _sigmoid(c), w_ref[...]) + b_ref[...]
    for j in range(o_ref.shape[1]):
        @pl.when(pl.program_id(0) == j)
        def _store():
            o_ref[:, j, :] = val


def _ada(c_all, w_ada, b_ada):
    n, d = c_all.shape
    nmod = w_ada.shape[1] // d
    return pl.pallas_call(
        _ada_kernel,
        out_shape=jax.ShapeDtypeStruct((n, nmod, d), F32),
        grid=(nmod,),
        in_specs=[pl.BlockSpec((n, d), lambda j: (0, 0)),
                  pl.BlockSpec((d, d), lambda j: (0, j)),
                  pl.BlockSpec((1, d), lambda j: (0, j))],
        out_specs=pl.BlockSpec((n, nmod, d), lambda j: (0, 0, 0)),
        compiler_params=pltpu.CompilerParams(dimension_semantics=("arbitrary",)),
        name="ada",
    )(c_all, w_ada, b_ada.reshape(1, nmod * d))


def _inproj_kernel(x_ref, mod_ref, g_ref, w_ref, oh_ref, or_ref, og_ref, *rest, ns):
    bb, tl, d = x_ref.shape
    if rest:
        wbf_ref, stage, sem = rest

        @pl.when((pl.program_id(0) == 0) & (pl.program_id(1) == 0))
        def _cast():
            _cast_weights([(w_ref, wbf_ref)], stage, sem)
        w_ref = wbf_ref

    def part(bsl, tsl):
        x = x_ref[bsl, tsl, :]
        nb_, nt_ = x.shape[0], x.shape[1]
        xn = x * lax.rsqrt(jnp.mean(x * x, axis=-1, keepdims=True) + RMS_EPS) * g_ref[...]
        xm = xn * (1.0 + mod_ref[bsl, 1:2, :]) + mod_ref[bsl, 0:1, :]
        xb = xm.reshape(nb_ * nt_, d).astype(BF16)
        yield
        lo = 0
        for o_ref in (oh_ref, or_ref, og_ref):
            w = o_ref.shape[-1]
            o_ref[bsl, tsl, :] = jnp.dot(xb, w_ref[:, lo:lo + w], preferred_element_type=F32).reshape(nb_, nt_, w)
            lo += w
            yield

    _run_staged(_row_parts(part, bb, tl, ns))


def _inproj(x, mod, mod_off, norm_g, w_in, bb, tl, ns):
    B, L, d = x.shape
    mod_blk = mod_off // bb
    const = lambda b, l: (0, 0)
    tok = lambda b, l: (b, l, 0)
    cast = w_in.dtype == F32
    out_shape = [jax.ShapeDtypeStruct((B, L, HG_W), F32),
                 jax.ShapeDtypeStruct((B, L, RB_PROJ_W), F32),
                 jax.ShapeDtypeStruct((B, L, GATE_W), F32)]
    out_specs = [pl.BlockSpec((bb, tl, HG_W), tok),
                 pl.BlockSpec((bb, tl, RB_PROJ_W), tok),
                 pl.BlockSpec((bb, tl, GATE_W), tok)]
    if cast:
        w_spec = pl.BlockSpec(memory_space=pl.ANY)
        out_shape.append(jax.ShapeDtypeStruct(w_in.shape, BF16))
        out_specs.append(pl.BlockSpec(w_in.shape, const, pipeline_mode=pl.Buffered(1)))
        scratch = [pltpu.VMEM((2,) + STAGE_SHAPE, F32), pltpu.SemaphoreType.DMA((2,))]
    else:
        w_spec = pl.BlockSpec((d, PROJ_W), const, pipeline_mode=pl.Buffered(1))
        scratch = []
    return pl.pallas_call(
        functools.partial(_inproj_kernel, ns=ns),
        out_shape=tuple(out_shape),
        grid=(B // bb, L // tl),
        in_specs=[pl.BlockSpec((bb, tl, d), tok),
                  pl.BlockSpec((bb, 6, d), lambda b, l: (b + mod_blk, 0, 0)),
                  pl.BlockSpec((1, d), const),
                  w_spec],
        out_specs=tuple(out_specs),
        scratch_shapes=scratch,
        compiler_params=pltpu.CompilerParams(dimension_semantics=("arbitrary", "arbitrary"),
                                             vmem_limit_bytes=VMEM_LIMIT_BYTES),
        name="inproj",
    )(x, mod, norm_g.reshape(1, d), w_in)


def _hgrn_kernel(ph_ref, s0_ref, lbp_ref, ng_ref, o_ref, s_ref, c2_s, *, layer, bb, nch, C, c):
    @pl.when(pl.program_id(1) == 0)
    def _init():
        s_ref[...] = s0_ref[...]

    hb = lbp_ref[...]
    e = jnp.exp(hb - jnp.max(hb, axis=0, keepdims=True))
    lb = jnp.sum(e[0:layer + 1], axis=0, keepdims=True) / jnp.sum(e, axis=0, keepdims=True)
    ng = ng_ref[...]
    cum_m = (_iota((C, C), 0) >= _iota((C, C), 1)).astype(BF16)
    nsub = C // c
    causal = _iota((C, C), 0) >= _iota((C, C), 1)

    def one_chunk(ci):
        r0 = pl.multiple_of(ci * C, C)
        tasks, units = [], []
        for u in range(bb):
            ph = ph_ref[u, pl.ds(r0, C), :]
            hq = ph[:, 0:HA_W]
            hf = ph[:, HA_W:2 * HA_W]
            hi = ph[:, 2 * HA_W:3 * HA_W]
            hg = ph[:, 3 * HA_W:4 * HA_W]
            q = hq * _sigmoid(hq)
            sg = _sigmoid(hf)
            logf = jnp.log(lb + (1.0 - lb) * sg)
            b2 = _sum3(cum_m, logf) * LOG2_E
            c2 = b2 - jnp.log2((1.0 - lb) * (1.0 - sg))
            c2_s[u] = c2
            unit = dict(u=u, gate=hg * _sigmoid(hg), tasks=[])
            for h in range(HA_HEADS):
                sl = slice(h * HA_DK, (h + 1) * HA_DK)
                t = dict(u=u, h=h, q=q[:, sl], v=hi[:, sl], b2=b2[:, sl], c2=c2[:, sl])
                tasks.append(t)
                unit['tasks'].append(t)
            units.append(unit)
        for t in tasks:
            t['S'] = s_ref[t['u'], t['h']]
        for t in tasks:
            t['o'] = _dot(t['q'] * jnp.exp2(t['b2']), t['S'])
        lane_is = [_iota((c, C), 1) == j for j in range(C)]
        for t in tasks:
            q, b2, c2 = t['q'], t['b2'], t['c2']
            sl = slice(t['h'] * HA_DK, (t['h'] + 1) * HA_DK)
            rows = []
            for i in range(nsub):
                lo = i * c
                qi, bi = q[lo:lo + c], b2[lo:lo + c]
                if i > 0:
                    bref = b2[lo - 1:lo, :]
                    kt = jnp.concatenate([jnp.exp2(bref - c2[:lo]), jnp.zeros((C - lo, HA_DK), F32)], axis=0)
                    sc = _dot_nt(qi * jnp.exp2(bi - bref), kt)
                else:
                    sc = jnp.zeros((c, C), F32)
                for s in range(c):
                    col = jnp.sum(qi * jnp.exp2(bi - c2_s[t['u'], lo + s:lo + s + 1, sl]), axis=-1, keepdims=True)
                    sc = jnp.where(lane_is[lo + s], col, sc)
                rows.append(sc)
            t['scores'] = jnp.where(causal, jnp.concatenate(rows, axis=0), 0.0)
        for t in tasks:
            t['o'] = t['o'] + _dot(t['scores'], t['v'])
        for t in tasks:
            bl = t['b2'][C - 1:C, :]
            kd = jnp.exp2(bl - t['c2'])
            dcol = jnp.transpose(jnp.broadcast_to(jnp.exp2(bl), (8, HA_DK)))[:, 0:1]
            s_ref[t['u'], t['h']] = t['S'] * dcol + _dot_tn(kd, t['v'])
        for unit in units:
            outs = [t['o'] * lax.rsqrt(jnp.mean(t['o'] * t['o'], axis=-1, keepdims=True) + RMS_EPS)
                    for t in unit['tasks']]
            o_ref[unit['u'], pl.ds(r0, C), :] = jnp.concatenate(outs, axis=1) * ng * unit['gate']

    def per_chunk(ci, carry):
        one_chunk(ci)
        return carry

    lax.fori_loop(0, nch, per_chunk, 0)


def _hgrn(ph, s0, lower_bounds, norm_g, layer, bb, tl, C, c):
    B, L, _ = ph.shape
    const = lambda b, l: (0, 0)
    st = lambda b, l: (b, 0, 0, 0)
    kern = functools.partial(_hgrn_kernel, layer=layer, bb=bb, nch=tl // C, C=C, c=c)
    return pl.pallas_call(
        kern,
        out_shape=(jax.ShapeDtypeStruct((B, L, HA_W), F32),
                   jax.ShapeDtypeStruct(s0.shape, F32)),
        grid=(B // bb, L // tl),
        in_specs=[pl.BlockSpec((bb, tl, HG_W), lambda b, l: (b, l, 0)),
                  pl.BlockSpec((bb, HA_HEADS, HA_DK, HA_DK), st),
                  pl.BlockSpec(lower_bounds.shape, const),
                  pl.BlockSpec((1, HA_W), const)],
        out_specs=(pl.BlockSpec((bb, tl, HA_W), lambda b, l: (b, l, 0)),
                   pl.BlockSpec((bb, HA_HEADS, HA_DK, HA_DK), st)),
        scratch_shapes=[pltpu.VMEM((bb, C, HA_W), F32)],
        compiler_params=pltpu.CompilerParams(dimension_semantics=("parallel", "arbitrary"),
                                             vmem_limit_bytes=VMEM_LIMIT_BYTES),
        name="hgrn",
    )(ph, s0, lower_bounds, norm_g.reshape(1, HA_W))


def _rwkv_kernel(pr_ref, prev_ref, sh0_ref, s0_ref, mu_ref, w0_ref, a0_ref, wwa_ref, g2_ref, kk_ref, ka_ref,
                 rk_ref, gnw_ref, gnb_ref, o_ref, s_ref, sh_ref, s2_s, *, bb, tl, C):
    l = pl.program_id(1)
    G = RWKV_GROUP
    nb = G // C
    npair = RB_HEADS // 2
    W = RB_PROJ_W

    @pl.when(l == 0)
    def _init():
        z = jnp.zeros((bb, RB_N, RB_N), F32)
        for p in range(npair):
            top = jnp.concatenate([s0_ref[:, 2 * p], z], axis=2)
            bot = jnp.concatenate([z, s0_ref[:, 2 * p + 1]], axis=2)
            s2_s[:, p] = jnp.concatenate([top, bot], axis=1)

    rg, cg = _iota((G, G), 0), _iota((G, G), 1)
    same_chunk = (rg // C) == (cg // C)
    cum_m = (same_chunk & (rg >= cg)).astype(BF16)
    if nb > 1:
        cum_m = jnp.concatenate([cum_m, same_chunk.astype(BF16)], axis=0)
    rp, cp = _iota((G, 2 * RB_N), 0), _iota((G, 2 * RB_N), 1) % RB_N
    same_c = (rp // C) == (cp // C)
    tri_s = same_c & ((rp % C) > (cp % C))
    tri_i = same_c & ((rp % C) >= (cp % C))
    head0 = _iota((G, 2 * RB_N), 1) < RB_N
    same_head = (_iota((2 * RB_N, 2 * RB_N), 0) // RB_N) == (_iota((2 * RB_N, 2 * RB_N), 1) // RB_N)
    seg = ((_iota((MXU_W, MXU_W), 0) // RB_N) == (_iota((MXU_W, MXU_W), 1) // RB_N)).astype(BF16)
    eye2 = jnp.where(rp == cp, 1.0, 0.0)
    is_tanh = _iota((G, RB_LORA_WA), 1) < RB_LORA_WA // 2
    nsq = int(math.log2(C)) - 1

    def stack2(x):
        return jnp.concatenate([jnp.where(head0, x, 0.0), jnp.where(head0, 0.0, x)], axis=0)

    if tl >= G:
        n_iter, n_unit = tl // G, bb
    else:
        n_unit = min(UNITS_PER_ITER, bb * tl // G)
        n_iter = bb * tl // (G * n_unit)
    sls = [slice(p * 2 * RB_N, (p + 1) * 2 * RB_N) for p in range(npair)]

    def load_rows(it, u):
        if tl >= G:
            r0 = pl.multiple_of(it * G, G)
            x = pr_ref[u, pl.ds(r0, G), :]
            before = pr_ref[u, pl.ds(jnp.maximum(r0 - 1, 0), 1), :]
            head = jnp.where(l == 0, sh0_ref[u], prev_ref[u, 7:8, :])
            first = jnp.where(it == 0, head, before)
            prev = jnp.where(_iota((G, W), 0) == 0, first, pltpu.roll(x, 1, 0))
            return x, prev, [u], (u, pl.ds(r0, G))
        b0 = pl.multiple_of((it * n_unit + u) * nb, nb)
        x = pr_ref[pl.ds(b0, nb)]
        rolled = pltpu.roll(x.reshape(G, W), 1, 0).reshape(nb, tl, W)
        prev = jnp.where(_iota((nb, tl, W), 1) == 0, sh0_ref[pl.ds(b0, nb)], rolled)
        return x.reshape(G, W), prev.reshape(G, W), [b0 + j for j in range(nb)], (pl.ds(b0, nb),)

    def step(it, carry):
        units = []
        for u in range(n_unit):
            x, xprev, bs, out_idx = load_rows(it, u)
            pm = x + (xprev - x) * mu_ref[...]
            dwa = pm[:, 3 * RB_W:3 * RB_W + RB_LORA_WA]
            units.append(dict(bs=bs, out_idx=out_idx, r=pm[:, 0:RB_W], k=pm[:, RB_W:2 * RB_W],
                              v=pm[:, 2 * RB_W:3 * RB_W], dwa=jnp.where(is_tanh, jnp.tanh(dwa), dwa),
                              sdg=_sigmoid(pm[:, 3 * RB_W + RB_LORA_WA:])))
        wa = _dot(jnp.concatenate([q['dwa'] for q in units], axis=0), wwa_ref[...])
        gate = _dot(jnp.concatenate([q['sdg'] for q in units], axis=0), g2_ref[...])
        for i, q in enumerate(units):
            rs = slice(i * G, (i + 1) * G)
            q['lw'] = -RB_DECAY_SCALE * _sigmoid(w0_ref[...] + wa[rs, 0:RB_W])
            q['a'] = _sigmoid(a0_ref[...] + wa[rs, RB_W:])
            q['gate'] = gate[rs]
            q['kk'] = q['k'] * kk_ref[...]
            q['kp'] = q['k'] * (1.0 + (q['a'] - 1.0) * ka_ref[...])
        sums = _segsum(jnp.concatenate([q['kk'] * q['kk'] for q in units]
                                       + [q['r'] * q['kp'] * rk_ref[...] for q in units], axis=0), seg)
        tasks = []
        for i, q in enumerate(units):
            kk = q['kk'] * lax.rsqrt(jnp.maximum(sums[i * G:(i + 1) * G], 1e-24))
            q['bonus'] = sums[(n_unit + i) * G:(n_unit + i + 1) * G] * q['v']
            kb = kk * q['a']
            lw, kp = q['lw'], q['kp']
            gs = _sum3(cum_m, lw)
            g = gs[:G]
            gl = gs[G:] if nb > 1 else g[G - 1:G, :]
            eg, eng, egl = jnp.exp(g), jnp.exp(-g), jnp.exp(gl)
            At = -kk * (eg * jnp.exp(-lw))
            Bt = kb * eng
            Kt = kp * eng
            Rt = q['r'] * eg
            dec = egl * eng
            Ke = kp * dec
            Be = kb * dec
            q['tasks'] = []
            for p in range(npair):
                sl = sls[p]
                t = dict(p=p, bs=q['bs'], AR=jnp.concatenate([At[:, sl], Rt[:, sl]], axis=0), B=Bt[:, sl],
                         K=Kt[:, sl], V=q['v'][:, sl], Ke=Ke[:, sl], Be=Be[:, sl], egl=egl[:, sl])
                tasks.append(t)
                q['tasks'].append(t)
        for t in tasks:
            t['S2'] = [s2_s[b, t['p']] for b in t['bs']]
        for t in tasks:
            t['M'] = _dot_nt(t['AR'], jnp.concatenate([stack2(t['B']), stack2(t['K'])], axis=0))
        for t in tasks:
            if nb == 1:
                x0 = _dot_nt(t['AR'], t['S2'][0])
                t['x0a'], t['x0r'] = x0[:G], x0[G:]
            else:
                x0 = [_dot_nt(jnp.concatenate([t['AR'][j * C:(j + 1) * C], t['AR'][G + j * C:G + (j + 1) * C]],
                                              axis=0), t['S2'][j]) for j in range(nb)]
                t['x0a'] = jnp.concatenate([x[:C] for x in x0], axis=0)
                t['x0r'] = jnp.concatenate([x[C:] for x in x0], axis=0)
        for t in tasks:
            M = t['M']
            lak = jnp.where(tri_s, M[:G, 2 * RB_N:], 0.0)
            lrk = jnp.where(tri_i, M[G:, 2 * RB_N:], 0.0)
            lv = _dot(jnp.concatenate([lak, lrk], axis=0), stack2(t['V']))
            t['X'] = t['x0a'] + lv[:G]
            t['Yv'] = t['x0r'] + lv[G:]
            t['P'] = jnp.where(tri_s, M[:G, :2 * RB_N], 0.0)
        for t in tasks:
            t['T'] = eye2 + t['P']
        for k in range(nsq):
            for t in tasks:
                rhs = stack2(t['P'])
                if k == 0:
                    t['P'] = _dot(t['P'], rhs)
                else:
                    pt = _dot(jnp.concatenate([t['P'], t['T']], axis=0), rhs)
                    t['P'] = pt[:G]
                    t['T'] = t['T'] + pt[G:]
        for t in tasks:
            t['T'] = t['T'] + _dot(t['T'], stack2(t['P']))
        for t in tasks:
            t['SA'] = _dot(t['T'], stack2(t['X']))
        for t in tasks:
            lrb = jnp.where(tri_i, t['M'][G:, :2 * RB_N], 0.0)
            t['y'] = t['Yv'] + _dot(lrb, stack2(t['SA']))
        for t in tasks:
            for j in range(nb):
                js = slice(j * C, (j + 1) * C)
                upd = _dot_tn(jnp.concatenate([t['V'][js], t['SA'][js]], axis=0),
                              jnp.concatenate([t['Ke'][js], t['Be'][js]], axis=0))
                s2_s[t['bs'][j], t['p']] = t['S2'][j] * t['egl'][j * C:j * C + 1, :] + jnp.where(same_head, upd, 0.0)
        y = jnp.concatenate([jnp.concatenate([t['y'] for t in q['tasks']], axis=1) for q in units], axis=0)
        yc = y - _segsum(y, seg) * (1.0 / RB_N)
        var = _segsum(yc * yc, seg) * (1.0 / RB_N)
        yn = yc * lax.rsqrt(var + RB_GN_EPS) * gnw_ref[...] + gnb_ref[...]
        for i, q in enumerate(units):
            out = (yn[i * G:(i + 1) * G] + q['bonus']) * q['gate']
            if tl >= G:
                o_ref[q['out_idx']] = out
            else:
                o_ref[q['out_idx']] = out.reshape(nb, tl, RB_W)
        return carry

    lax.fori_loop(0, n_iter, step, 0)

    @pl.when(l == pl.num_programs(1) - 1)
    def _fin():
        sh_ref[...] = pr_ref[:, tl - 1:tl, :]
        for p in range(npair):
            S2 = s2_s[:, p]
            s_ref[:, 2 * p] = S2[:, :RB_N, :RB_N]
            s_ref[:, 2 * p + 1] = S2[:, RB_N:, RB_N:]


def _rwkv(pr, shift0, s0, mu, w0, a0, wwa, g2, k_k, k_a, r_k, gn_w, gn_b, bb, tl, C):
    B, L, _ = pr.shape
    const = lambda b, l: (0, 0)
    tok = lambda b, l: (b, l, 0)
    st = lambda b, l: (b, 0, 0, 0)
    vec = lambda x: x.reshape(1, -1)
    kern = functools.partial(_rwkv_kernel, bb=bb, tl=tl, C=C)
    return pl.pallas_call(
        kern,
        out_shape=(jax.ShapeDtypeStruct((B, L, RB_W), F32),
                   jax.ShapeDtypeStruct(s0.shape, F32),
                   jax.ShapeDtypeStruct((B, 1, RB_PROJ_W), F32)),
        grid=(B // bb, L // tl),
        in_specs=[pl.BlockSpec((bb, tl, RB_PROJ_W), tok),
                  pl.BlockSpec((bb, 8, RB_PROJ_W), lambda b, l: (b, jnp.maximum(l * (tl // 8) - 1, 0), 0)),
                  pl.BlockSpec((bb, 1, RB_PROJ_W), lambda b, l: (b, 0, 0)),
                  pl.BlockSpec((bb, RB_HEADS, RB_N, RB_N), st),
                  pl.BlockSpec((1, RB_PROJ_W), const),
                  pl.BlockSpec((1, RB_W), const),
                  pl.BlockSpec((1, RB_W), const),
                  pl.BlockSpec((RB_LORA_WA, 2 * RB_W), const),
                  pl.BlockSpec((RB_LORA_G, RB_W), const),
                  pl.BlockSpec((1, RB_W), const),
                  pl.BlockSpec((1, RB_W), const),
                  pl.BlockSpec((1, RB_W), const),
                  pl.BlockSpec((1, RB_W), const),
                  pl.BlockSpec((1, RB_W), const)],
        out_specs=(pl.BlockSpec((bb, tl, RB_W), tok),
                   pl.BlockSpec((bb, RB_HEADS, RB_N, RB_N), st),
                   pl.BlockSpec((bb, 1, RB_PROJ_W), lambda b, l: (b, 0, 0))),
        scratch_shapes=[pltpu.VMEM((bb, RB_HEADS // 2, 2 * RB_N, 2 * RB_N), F32)],
        compiler_params=pltpu.CompilerParams(dimension_semantics=("parallel", "arbitrary"),
                                             vmem_limit_bytes=VMEM_LIMIT_BYTES),
        name="rwkv",
    )(pr, pr, shift0.reshape(B, 1, RB_PROJ_W), s0, vec(mu), vec(w0), vec(a0), wwa, g2, vec(k_k), vec(k_a),
      vec(r_k), vec(gn_w), vec(gn_b))


def _mixffn_kernel(oa_ref, ob_ref, pg_ref, x_ref, mod_ref, nf_ref, nfin_ref, wua_ref, wub_ref, wo_ref, wfi_ref,
                   wfo_ref, y_ref, *rest, ns):
    bb, tl, d = x_ref.shape
    if rest:
        wbf, (stage, sem) = rest[:5], rest[5:]

        @pl.when((pl.program_id(0) == 0) & (pl.program_id(1) == 0))
        def _cast():
            _cast_weights(list(zip((wua_ref, wub_ref, wo_ref, wfi_ref, wfo_ref), wbf)), stage, sem)
        wua_ref, wub_ref, wo_ref, wfi_ref, wfo_ref = wbf

    def part(bsl, tsl):
        rows = lambda ref: ref[bsl, tsl, :]
        x = rows(x_ref)
        nb_, nt_ = x.shape[0], x.shape[1]
        R = nb_ * nt_
        mod = lambda i: mod_ref[bsl, i:i + 1, :]
        pg = rows(pg_ref).reshape(R, GATE_W)
        ua = _dot(rows(oa_ref).reshape(R, HA_W), wua_ref[...])
        ub = _dot(rows(ob_ref).reshape(R, RB_W), wub_ref[...])
        yield
        merged = _sigmoid(pg[:, :d]) * ua + _sigmoid(pg[:, d:]) * ub
        mix = _dot(merged, wo_ref[...]).reshape(nb_, nt_, d)
        yield
        x1 = x + mod(2) * mix
        xn = x1 * lax.rsqrt(jnp.mean(x1 * x1, axis=-1, keepdims=True) + RMS_EPS) * nf_ref[...]
        xf = (xn * (1.0 + mod(4)) + mod(3)).reshape(R, d)
        h = _dot(xf, wfi_ref[...])
        yield
        gate, up = h[:, :FFN_HIDDEN], h[:, FFN_HIDDEN:]
        act = gate * _sigmoid(gate) * up
        f = _dot(act, wfo_ref[...]).reshape(nb_, nt_, d)
        yield
        x2 = x1 + mod(5) * f
        y_ref[bsl, tsl, :] = x2 * lax.rsqrt(jnp.mean(x2 * x2, axis=-1, keepdims=True) + RMS_EPS) * nfin_ref[...]

    _run_staged(_row_parts(part, bb, tl, ns))


def _mixffn(oa, ob, pg, x, mod, mod_off, norm_ffn, norm_final, weights, bb, tl, ns):
    B, L, d = x.shape
    mod_blk = mod_off // bb
    const = lambda b, l: (0, 0)
    tok = lambda b, l: (b, l, 0)
    cast = weights[0].dtype == F32
    out_shape = [jax.ShapeDtypeStruct((B, L, d), F32)]
    out_specs = [pl.BlockSpec((bb, tl, d), tok)]
    if cast:
        w_specs = [pl.BlockSpec(memory_space=pl.ANY) for _ in weights]
        out_shape += [jax.ShapeDtypeStruct(w.shape, BF16) for w in weights]
        out_specs += [pl.BlockSpec(w.shape, const, pipeline_mode=pl.Buffered(1)) for w in weights]
        scratch = [pltpu.VMEM((2,) + STAGE_SHAPE, F32), pltpu.SemaphoreType.DMA((2,))]
    else:
        w_specs = [pl.BlockSpec(w.shape, const, pipeline_mode=pl.Buffered(1)) for w in weights]
        scratch = []
    return pl.pallas_call(
        functools.partial(_mixffn_kernel, ns=ns),
        out_shape=tuple(out_shape),
        grid=(B // bb, L // tl),
        in_specs=[pl.BlockSpec((bb, tl, HA_W), tok),
                  pl.BlockSpec((bb, tl, RB_W), tok),
                  pl.BlockSpec((bb, tl, GATE_W), tok),
                  pl.BlockSpec((bb, tl, d), tok),
                  pl.BlockSpec((bb, 6, d), lambda b, l: (b + mod_blk, 0, 0)),
                  pl.BlockSpec((1, d), const),
                  pl.BlockSpec((1, d), const)] + w_specs,
        out_specs=tuple(out_specs),
        scratch_shapes=scratch,
        compiler_params=pltpu.CompilerParams(dimension_semantics=("arbitrary", "arbitrary"),
                                             vmem_limit_bytes=VMEM_LIMIT_BYTES),
        name="mixffn",
    )(oa, ob, pg, x, mod, norm_ffn.reshape(1, d), norm_final.reshape(1, d), *weights)


def _block(x, mod, mod_off, s_h, s_r, shift, p, w_in, w_mix, layer, tiles):
    assert mod_off % tiles['inproj'][0] == 0 and mod_off % tiles['mixffn'][0] == 0
    ph, pr, pg, *w_in_bf = _inproj(x, mod, mod_off, p['norm_mix'], w_in, *tiles['inproj'])
    oa, s_h = _hgrn(ph, s_h, p['lower_bounds'], p['hgrn_norm'], layer, *tiles['hgrn'])
    ob, s_r, shift = _rwkv(pr, shift, s_r, p['mu'], p['w0'], p['a0'], p['wwa'], p['g2'], p['k_k'], p['k_a'], p['r_k'],
                           p['gn_w'], p['gn_b'], *tiles['rwkv'])
    y, *w_mix_bf = _mixffn(oa, ob, pg, x, mod, mod_off, p['norm_ffn'], p['norm_final'], w_mix, *tiles['mixffn'])
    return (y, s_h, s_r, shift.reshape(shift.shape[0], RB_PROJ_W)), (w_in_bf[0] if w_in_bf else w_in), (w_mix_bf or w_mix)


PROMPT_TILES = {'inproj': (1, 512, 4), 'mixffn': (1, 512, 2), 'hgrn': (8, 128, 64, 8), 'rwkv': (4, 128, 64)}
SAMPLE_TILES = {'inproj': (64, 8, 2), 'mixffn': (64, 8, 2), 'hgrn': (16, 8, 8, 8), 'rwkv': (32, 8, 8)}


def kernel(x_prompt, x_sample, c_prompt, c_sample, state_hgrn, state_rwkv, state_shift, w_ada, b_ada, norm_mix, norm_ffn, w_in, hgrn_lower_bounds, hgrn_norm, rwkv_mu, rwkv_w0, rwkv_w2, rwkv_a0, rwkv_a2, rwkv_g2, rwkv_k_k, rwkv_k_a, rwkv_r_k, rwkv_gn_w, rwkv_gn_b, w_up_a, w_up_b, w_out, w_ffn_in, w_ffn_out, norm_final):
    depth = w_ada.shape[0]
    assert depth == 1, "the final norm is fused into the layer kernel"
    nbp, nbs = x_prompt.shape[0], x_sample.shape[0]

    hp = jnp.zeros((nbp, HA_HEADS, HA_DK, HA_DK), F32)
    rp = jnp.zeros((nbp, RB_HEADS, RB_N, RB_N), F32)
    sp = jnp.zeros((nbp, RB_PROJ_W), x_prompt.dtype)

    yp, ys = x_prompt, x_sample
    outs = [[] for _ in range(6)]
    for l in range(depth):
        half = RB_LORA_WA // 2
        zero = jnp.zeros((half, RB_W), F32)
        wwa = jnp.concatenate([jnp.concatenate([rwkv_w2[l], zero], axis=1),
                               jnp.concatenate([zero, rwkv_a2[l]], axis=1)], axis=0)
        p = {
            'norm_mix': norm_mix[l], 'norm_ffn': norm_ffn[l], 'norm_final': norm_final,
            'lower_bounds': hgrn_lower_bounds, 'hgrn_norm': hgrn_norm[l],
            'mu': rwkv_mu[l], 'w0': rwkv_w0[l], 'a0': rwkv_a0[l], 'wwa': wwa.astype(BF16),
            'g2': rwkv_g2[l].astype(BF16), 'k_k': rwkv_k_k[l], 'k_a': rwkv_k_a[l], 'r_k': rwkv_r_k[l],
            'gn_w': rwkv_gn_w[l], 'gn_b': rwkv_gn_b[l],
        }
        w_mix = (w_up_a[l], w_up_b[l], w_out[l], w_ffn_in[l], w_ffn_out[l])
        mod = _ada(jnp.concatenate([c_sample, c_prompt], axis=0), w_ada[l], b_ada[l])
        (yp, h1, r1, s1), w_in_bf, w_mix_bf = _block(yp, mod, nbs, hp, rp, sp, p, w_in[l], w_mix, l, PROMPT_TILES)
        (ys, h2, r2, s2), _, _ = _block(ys, mod, 0, state_hgrn[l], state_rwkv[l], state_shift[l], p, w_in_bf,
                                        tuple(w_mix_bf), l, SAMPLE_TILES)
        for lst, val in zip(outs, (h1, r1, s1, h2, r2, s2)):
            lst.append(val)
    stacked = [jnp.stack(o) for o in outs]
    return (yp, ys, *stacked)
```

```python
import functools
import math

import jax
import jax.numpy as jnp
from jax import lax
from jax.experimental import pallas as pl
from jax.experimental.pallas import tpu as pltpu

F32 = jnp.float32
BF16 = jnp.bfloat16

D_MODEL = 1024
HA_HEADS = 4
HA_DK = 128
HA_W = HA_HEADS * HA_DK
RB_HEADS = 8
RB_N = 64
RB_W = RB_HEADS * RB_N
RB_LORA_WA = 128
RB_LORA_G = 128
RB_PROJ_W = 3 * RB_W + RB_LORA_WA + RB_LORA_G
RB_DECAY_SCALE = math.exp(-0.5)
RB_GN_EPS = 64e-5
HG_W = 4 * HA_W
GATE_W = 2 * D_MODEL
PROJ_W = HG_W + RB_PROJ_W + GATE_W
FFN_HIDDEN = 2816
RMS_EPS = 1e-6
LOG2_E = math.log2(math.e)

MXU_W = 256
VMEM_LIMIT_BYTES = 58 * 1024 * 1024
STAGE_SHAPE = (4, 1024, 256)
UNITS_PER_ITER = 4
RWKV_GROUP = 64


def _dot(a, b):
    return jnp.dot(a.astype(BF16), b.astype(BF16), preferred_element_type=F32)


def _dot_nt(a, b):
    return lax.dot_general(a.astype(BF16), b.astype(BF16), (((1,), (1,)), ((), ())), preferred_element_type=F32)


def _dot_tn(a, b):
    return lax.dot_general(a.astype(BF16), b.astype(BF16), (((0,), (0,)), ((), ())), preferred_element_type=F32)


def _sigmoid(x):
    return 1.0 / (1.0 + jnp.exp(-x))


def _segsum(x, seg):
    w = seg.shape[0]
    xb = x.astype(BF16)
    return jnp.concatenate([jnp.dot(xb[:, i:i + w], seg, preferred_element_type=F32)
                            for i in range(0, x.shape[1], w)], axis=1)


def _sum3(m, x):
    hi = x.astype(BF16)
    r1 = x - hi.astype(F32)
    mid = r1.astype(BF16)
    lo = (r1 - mid.astype(F32)).astype(BF16)
    return (jnp.dot(m, hi, preferred_element_type=F32) + jnp.dot(m, mid, preferred_element_type=F32)
            + jnp.dot(m, lo, preferred_element_type=F32))


def _run_staged(gens):
    gens = list(gens)
    while gens:
        for g in list(gens):
            try:
                next(g)
            except StopIteration:
                gens.remove(g)


def _row_parts(part, bb, tl, ns):
    if bb == 1:
        step = tl // ns
        return [part(slice(None), slice(i * step, (i + 1) * step)) for i in range(ns)]
    step = bb // ns
    return [part(slice(i * step, (i + 1) * step), slice(None)) for i in range(ns)]


def _cast_weights(pairs, stage, sem):
    nslot, srows, scols = stage.shape
    chunks = []
    for src, dst in pairs:
        nrow, ncol = src.shape
        for r0 in range(0, nrow, srows):
            for c0 in range(0, ncol, scols):
                chunks.append((src, dst, r0, min(srows, nrow - r0), c0, min(scols, ncol - c0)))

    def copy(i):
        src, _, r0, nr, c0, nc = chunks[i]
        return pltpu.make_async_copy(src.at[pl.ds(r0, nr), pl.ds(c0, nc)],
                                     stage.at[i % nslot, pl.ds(0, nr), pl.ds(0, nc)], sem.at[i % nslot])

    for i in range(min(nslot - 1, len(chunks))):
        copy(i).start()
    for i, (_, dst, r0, nr, c0, nc) in enumerate(chunks):
        if i + nslot - 1 < len(chunks):
            copy(i + nslot - 1).start()
        copy(i).wait()
        dst[r0:r0 + nr, c0:c0 + nc] = stage[i % nslot, 0:nr, 0:nc].astype(BF16)


def _iota(shape, axis):
    return lax.broadcasted_iota(jnp.int32, shape, axis)


def _ada_kernel(c_ref, w_ref, b_ref, o_ref):
    c = c_ref[...]
    val = _dot(c * _sigmoid(c), w_ref[...]) + b_ref[...]
    for j in range(o_ref.shape[1]):
        @pl.when(pl.program_id(0) == j)
        def _store():
            o_ref[:, j, :] = val


def _ada(c_all, w_ada, b_ada):
    n, d = c_all.shape
    nmod = w_ada.shape[1] // d
    return pl.pallas_call(
        _ada_kernel,
        out_shape=jax.ShapeDtypeStruct((n, nmod, d), F32),
        grid=(nmod,),
        in_specs=[pl.BlockSpec((n, d), lambda j: (0, 0)),
                  pl.BlockSpec((d, d), lambda j: (0, j)),
                  pl.BlockSpec((1, d), lambda j: (0, j))],
        out_specs=pl.BlockSpec((n, nmod, d), lambda j: (0, 0, 0)),
        compiler_params=pltpu.CompilerParams(dimension_semantics=("arbitrary",)),
        name="ada",
    )(c_all, w_ada, b_ada.reshape(1, nmod * d))


def _inproj_kernel(x_ref, mod_ref, g_ref, w_ref, oh_ref, or_ref, og_ref, *rest, ns):
    bb, tl, d = x_ref.shape
    if rest:
        wbf_ref, stage, sem = rest

        @pl.when((pl.program_id(0) == 0) & (pl.program_id(1) == 0))
        def _cast():
            _cast_weights([(w_ref, wbf_ref)], stage, sem)
        w_ref = wbf_ref

    def part(bsl, tsl):
        x = x_ref[bsl, tsl, :]
        nb_, nt_ = x.shape[0], x.shape[1]
        xn = x * lax.rsqrt(jnp.mean(x * x, axis=-1, keepdims=True) + RMS_EPS) * g_ref[...]
        xm = xn * (1.0 + mod_ref[bsl, 1:2, :]) + mod_ref[bsl, 0:1, :]
        xb = xm.reshape(nb_ * nt_, d).astype(BF16)
        yield
        lo = 0
        for o_ref in (oh_ref, or_ref, og_ref):
            w = o_ref.shape[-1]
            o_ref[bsl, tsl, :] = jnp.dot(xb, w_ref[:, lo:lo + w], preferred_element_type=F32).reshape(nb_, nt_, w)
            lo += w
            yield

    _run_staged(_row_parts(part, bb, tl, ns))


def _inproj(x, mod, mod_off, norm_g, w_in, bb, tl, ns):
    B, L, d = x.shape
    mod_blk = mod_off // bb
    const = lambda b, l: (0, 0)
    tok = lambda b, l: (b, l, 0)
    cast = w_in.dtype == F32
    out_shape = [jax.ShapeDtypeStruct((B, L, HG_W), F32),
                 jax.ShapeDtypeStruct((B, L, RB_PROJ_W), F32),
                 jax.ShapeDtypeStruct((B, L, GATE_W), F32)]
    out_specs = [pl.BlockSpec((bb, tl, HG_W), tok),
                 pl.BlockSpec((bb, tl, RB_PROJ_W), tok),
                 pl.BlockSpec((bb, tl, GATE_W), tok)]
    if cast:
        w_spec = pl.BlockSpec(memory_space=pl.ANY)
        out_shape.append(jax.ShapeDtypeStruct(w_in.shape, BF16))
        out_specs.append(pl.BlockSpec(w_in.shape, const, pipeline_mode=pl.Buffered(1)))
        scratch = [pltpu.VMEM(STAGE_SHAPE, F32), pltpu.SemaphoreType.DMA((STAGE_SHAPE[0],))]
    else:
        w_spec = pl.BlockSpec((d, PROJ_W), const, pipeline_mode=pl.Buffered(1))
        scratch = []
    return pl.pallas_call(
        functools.partial(_inproj_kernel, ns=ns),
        out_shape=tuple(out_shape),
        grid=(B // bb, L // tl),
        in_specs=[pl.BlockSpec((bb, tl, d), tok),
                  pl.BlockSpec((bb, 6, d), lambda b, l: (b + mod_blk, 0, 0)),
                  pl.BlockSpec((1, d), const),
                  w_spec],
        out_specs=tuple(out_specs),
        scratch_shapes=scratch,
        compiler_params=pltpu.CompilerParams(dimension_semantics=("arbitrary", "arbitrary"),
                                             vmem_limit_bytes=VMEM_LIMIT_BYTES),
        name="inproj",
    )(x, mod, norm_g.reshape(1, d), w_in)


def _hgrn_kernel(ph_ref, s0_ref, lbp_ref, ng_ref, o_ref, s_ref, c2_s, *, layer, bb, nch, C, c):
    @pl.when(pl.program_id(1) == 0)
    def _init():
        s_ref[...] = s0_ref[...]

    hb = lbp_ref[...]
    e = jnp.exp(hb - jnp.max(hb, axis=0, keepdims=True))
    lb = jnp.sum(e[0:layer + 1], axis=0, keepdims=True) / jnp.sum(e, axis=0, keepdims=True)
    ng = ng_ref[...]
    cum_m = (_iota((C, C), 0) >= _iota((C, C), 1)).astype(BF16)
    nsub = C // c
    causal = _iota((C, C), 0) >= _iota((C, C), 1)

    def one_chunk(ci):
        r0 = pl.multiple_of(ci * C, C)
        tasks, units = [], []
        for u in range(bb):
            ph = ph_ref[u, pl.ds(r0, C), :]
            hq = ph[:, 0:HA_W]
            hf = ph[:, HA_W:2 * HA_W]
            hi = ph[:, 2 * HA_W:3 * HA_W]
            hg = ph[:, 3 * HA_W:4 * HA_W]
            q = hq * _sigmoid(hq)
            sg = _sigmoid(hf)
            logf = jnp.log(lb + (1.0 - lb) * sg)
            b2 = _sum3(cum_m, logf) * LOG2_E
            c2 = b2 - jnp.log2((1.0 - lb) * (1.0 - sg))
            c2_s[u] = c2
            unit = dict(u=u, gate=hg * _sigmoid(hg), tasks=[])
            for h in range(HA_HEADS):
                sl = slice(h * HA_DK, (h + 1) * HA_DK)
                t = dict(u=u, h=h, q=q[:, sl], v=hi[:, sl], b2=b2[:, sl], c2=c2[:, sl])
                tasks.append(t)
                unit['tasks'].append(t)
            units.append(unit)
        for t in tasks:
            t['S'] = s_ref[t['u'], t['h']]
        for t in tasks:
            t['o'] = _dot(t['q'] * jnp.exp2(t['b2']), t['S'])
        lane_is = [_iota((c, C), 1) == j for j in range(C)]
        for t in tasks:
            q, b2, c2 = t['q'], t['b2'], t['c2']
            sl = slice(t['h'] * HA_DK, (t['h'] + 1) * HA_DK)
            rows = []
            for i in range(nsub):
                lo = i * c
                qi, bi = q[lo:lo + c], b2[lo:lo + c]
                if i > 0:
                    bref = b2[lo - 1:lo, :]
                    kt = jnp.concatenate([jnp.exp2(bref - c2[:lo]), jnp.zeros((C - lo, HA_DK), F32)], axis=0)
                    sc = _dot_nt(qi * jnp.exp2(bi - bref), kt)
                else:
                    sc = jnp.zeros((c, C), F32)
                for s in range(c):
                    col = jnp.sum(qi * jnp.exp2(bi - c2_s[t['u'], lo + s:lo + s + 1, sl]), axis=-1, keepdims=True)
                    sc = jnp.where(lane_is[lo + s], col, sc)
                rows.append(sc)
            t['scores'] = jnp.where(causal, jnp.concatenate(rows, axis=0), 0.0)
        for t in tasks:
            t['o'] = t['o'] + _dot(t['scores'], t['v'])
        for t in tasks:
            bl = t['b2'][C - 1:C, :]
            kd = jnp.exp2(bl - t['c2'])
            dcol = jnp.transpose(jnp.broadcast_to(jnp.exp2(bl), (8, HA_DK)))[:, 0:1]
            s_ref[t['u'], t['h']] = t['S'] * dcol + _dot_tn(kd, t['v'])
        for unit in units:
            outs = [t['o'] * lax.rsqrt(jnp.mean(t['o'] * t['o'], axis=-1, keepdims=True) + RMS_EPS)
                    for t in unit['tasks']]
            o_ref[unit['u'], pl.ds(r0, C), :] = jnp.concatenate(outs, axis=1) * ng * unit['gate']

    def per_chunk(ci, carry):
        one_chunk(ci)
        return carry

    lax.fori_loop(0, nch, per_chunk, 0)


def _hgrn(ph, s0, lower_bounds, norm_g, layer, bb, tl, C, c):
    B, L, _ = ph.shape
    const = lambda b, l: (0, 0)
    st = lambda b, l: (b, 0, 0, 0)
    kern = functools.partial(_hgrn_kernel, layer=layer, bb=bb, nch=tl // C, C=C, c=c)
    return pl.pallas_call(
        kern,
        out_shape=(jax.ShapeDtypeStruct((B, L, HA_W), F32),
                   jax.ShapeDtypeStruct(s0.shape, F32)),
        grid=(B // bb, L // tl),
        in_specs=[pl.BlockSpec((bb, tl, HG_W), lambda b, l: (b, l, 0)),
                  pl.BlockSpec((bb, HA_HEADS, HA_DK, HA_DK), st),
                  pl.BlockSpec(lower_bounds.shape, const),
                  pl.BlockSpec((1, HA_W), const)],
        out_specs=(pl.BlockSpec((bb, tl, HA_W), lambda b, l: (b, l, 0)),
                   pl.BlockSpec((bb, HA_HEADS, HA_DK, HA_DK), st)),
        scratch_shapes=[pltpu.VMEM((bb, C, HA_W), F32)],
        compiler_params=pltpu.CompilerParams(dimension_semantics=("parallel", "arbitrary"),
                                             vmem_limit_bytes=VMEM_LIMIT_BYTES),
        name="hgrn",
    )(ph, s0, lower_bounds, norm_g.reshape(1, HA_W))


def _rwkv_kernel(pr_ref, prev_ref, sh0_ref, s0_ref, mu_ref, w0_ref, a0_ref, wwa_ref, g2_ref, kk_ref, ka_ref,
                 rk_ref, gnw_ref, gnb_ref, o_ref, s_ref, sh_ref, s2_s, *, bb, tl, C):
    l = pl.program_id(1)
    G = RWKV_GROUP
    nb = G // C
    npair = RB_HEADS // 2
    W = RB_PROJ_W

    @pl.when(l == 0)
    def _init():
        z = jnp.zeros((bb, RB_N, RB_N), F32)
        for p in range(npair):
            top = jnp.concatenate([s0_ref[:, 2 * p], z], axis=2)
            bot = jnp.concatenate([z, s0_ref[:, 2 * p + 1]], axis=2)
            s2_s[:, p] = jnp.concatenate([top, bot], axis=1)

    rg, cg = _iota((G, G), 0), _iota((G, G), 1)
    same_chunk = (rg // C) == (cg // C)
    cum_m = (same_chunk & (rg >= cg)).astype(BF16)
    if nb > 1:
        cum_m = jnp.concatenate([cum_m, same_chunk.astype(BF16)], axis=0)
    rp, cp = _iota((G, 2 * RB_N), 0), _iota((G, 2 * RB_N), 1) % RB_N
    same_c = (rp // C) == (cp // C)
    tri_s = same_c & ((rp % C) > (cp % C))
    tri_i = same_c & ((rp % C) >= (cp % C))
    head0 = _iota((G, 2 * RB_N), 1) < RB_N
    same_head = (_iota((2 * RB_N, 2 * RB_N), 0) // RB_N) == (_iota((2 * RB_N, 2 * RB_N), 1) // RB_N)
    seg = ((_iota((MXU_W, MXU_W), 0) // RB_N) == (_iota((MXU_W, MXU_W), 1) // RB_N)).astype(BF16)
    eye2 = jnp.where(rp == cp, 1.0, 0.0)
    is_tanh = _iota((G, RB_LORA_WA), 1) < RB_LORA_WA // 2
    nsq = int(math.log2(C)) - 1

    def stack2(x):
        return jnp.concatenate([jnp.where(head0, x, 0.0), jnp.where(head0, 0.0, x)], axis=0)

    if tl >= G:
        n_iter, n_unit = tl // G, bb
    else:
        n_unit = min(UNITS_PER_ITER, bb * tl // G)
        n_iter = bb * tl // (G * n_unit)
    sls = [slice(p * 2 * RB_N, (p + 1) * 2 * RB_N) for p in range(npair)]

    def load_rows(it, u):
        if tl >= G:
            r0 = pl.multiple_of(it * G, G)
            x = pr_ref[u, pl.ds(r0, G), :]
            before = pr_ref[u, pl.ds(jnp.maximum(r0 - 1, 0), 1), :]
            head = jnp.where(l == 0, sh0_ref[u], prev_ref[u, 7:8, :])
            first = jnp.where(it == 0, head, before)
            prev = jnp.where(_iota((G, W), 0) == 0, first, pltpu.roll(x, 1, 0))
            return x, prev, [u], (u, pl.ds(r0, G))
        b0 = pl.multiple_of((it * n_unit + u) * nb, nb)
        x = pr_ref[pl.ds(b0, nb)]
        rolled = pltpu.roll(x.reshape(G, W), 1, 0).reshape(nb, tl, W)
        prev = jnp.where(_iota((nb, tl, W), 1) == 0, sh0_ref[pl.ds(b0, nb)], rolled)
        return x.reshape(G, W), prev.reshape(G, W), [b0 + j for j in range(nb)], (pl.ds(b0, nb),)

    def step(it, carry):
        units = []
        for u in range(n_unit):
            x, xprev, bs, out_idx = load_rows(it, u)
            pm = x + (xprev - x) * mu_ref[...]
            dwa = pm[:, 3 * RB_W:3 * RB_W + RB_LORA_WA]
            units.append(dict(bs=bs, out_idx=out_idx, r=pm[:, 0:RB_W], k=pm[:, RB_W:2 * RB_W],
                              v=pm[:, 2 * RB_W:3 * RB_W], dwa=jnp.where(is_tanh, jnp.tanh(dwa), dwa),
                              sdg=_sigmoid(pm[:, 3 * RB_W + RB_LORA_WA:])))
        wa = _dot(jnp.concatenate([q['dwa'] for q in units], axis=0), wwa_ref[...])
        gate = _dot(jnp.concatenate([q['sdg'] for q in units], axis=0), g2_ref[...])
        for i, q in enumerate(units):
            rs = slice(i * G, (i + 1) * G)
            q['lw'] = -RB_DECAY_SCALE * _sigmoid(w0_ref[...] + wa[rs, 0:RB_W])
            q['a'] = _sigmoid(a0_ref[...] + wa[rs, RB_W:])
            q['gate'] = gate[rs]
            q['kk'] = q['k'] * kk_ref[...]
            q['kp'] = q['k'] * (1.0 + (q['a'] - 1.0) * ka_ref[...])
        sums = _segsum(jnp.concatenate([q['kk'] * q['kk'] for q in units]
                                       + [q['r'] * q['kp'] * rk_ref[...] for q in units], axis=0), seg)
        tasks = []
        for i, q in enumerate(units):
            kk = q['kk'] * lax.rsqrt(jnp.maximum(sums[i * G:(i + 1) * G], 1e-24))
            q['bonus'] = sums[(n_unit + i) * G:(n_unit + i + 1) * G] * q['v']
            kb = kk * q['a']
            lw, kp = q['lw'], q['kp']
            gs = _sum3(cum_m, lw)
            g = gs[:G]
            gl = gs[G:] if nb > 1 else g[G - 1:G, :]
            eg, eng, egl = jnp.exp(g), jnp.exp(-g), jnp.exp(gl)
            At = -kk * (eg * jnp.exp(-lw))
            Bt = kb * eng
            Kt = kp * eng
            Rt = q['r'] * eg
            dec = egl * eng
            Ke = kp * dec
            Be = kb * dec
            q['tasks'] = []
            for p in range(npair):
                sl = sls[p]
                t = dict(p=p, bs=q['bs'], AR=jnp.concatenate([At[:, sl], Rt[:, sl]], axis=0), B=Bt[:, sl],
                         K=Kt[:, sl], V=q['v'][:, sl], Ke=Ke[:, sl], Be=Be[:, sl], egl=egl[:, sl])
                tasks.append(t)
                q['tasks'].append(t)
        for t in tasks:
            t['S2'] = [s2_s[b, t['p']] for b in t['bs']]
        for t in tasks:
            t['M'] = _dot_nt(t['AR'], jnp.concatenate([stack2(t['B']), stack2(t['K'])], axis=0))
        for t in tasks:
            if nb == 1:
                x0 = _dot_nt(t['AR'], t['S2'][0])
                t['x0a'], t['x0r'] = x0[:G], x0[G:]
            else:
                x0 = [_dot_nt(jnp.concatenate([t['AR'][j * C:(j + 1) * C], t['AR'][G + j * C:G + (j + 1) * C]],
                                              axis=0), t['S2'][j]) for j in range(nb)]
                t['x0a'] = jnp.concatenate([x[:C] for x in x0], axis=0)
                t['x0r'] = jnp.concatenate([x[C:] for x in x0], axis=0)
        for t in tasks:
            M = t['M']
            lak = jnp.where(tri_s, M[:G, 2 * RB_N:], 0.0)
            lrk = jnp.where(tri_i, M[G:, 2 * RB_N:], 0.0)
            lv = _dot(jnp.concatenate([lak, lrk], axis=0), stack2(t['V']))
            t['X'] = t['x0a'] + lv[:G]
            t['Yv'] = t['x0r'] + lv[G:]
            t['P'] = jnp.where(tri_s, M[:G, :2 * RB_N], 0.0)
        for t in tasks:
            t['T'] = eye2 + t['P']
        for k in range(nsq):
            for t in tasks:
                rhs = stack2(t['P'])
                if k == 0:
                    t['P'] = _dot(t['P'], rhs)
                else:
                    pt = _dot(jnp.concatenate([t['P'], t['T']], axis=0), rhs)
                    t['P'] = pt[:G]
                    t['T'] = t['T'] + pt[G:]
        for t in tasks:
            t['T'] = t['T'] + _dot(t['T'], stack2(t['P']))
        for t in tasks:
            t['SA'] = _dot(t['T'], stack2(t['X']))
        for t in tasks:
            lrb = jnp.where(tri_i, t['M'][G:, :2 * RB_N], 0.0)
            t['y'] = t['Yv'] + _dot(lrb, stack2(t['SA']))
        for t in tasks:
            for j in range(nb):
                js = slice(j * C, (j + 1) * C)
                upd = _dot_tn(jnp.concatenate([t['V'][js], t['SA'][js]], axis=0),
                              jnp.concatenate([t['Ke'][js], t['Be'][js]], axis=0))
                s2_s[t['bs'][j], t['p']] = t['S2'][j] * t['egl'][j * C:j * C + 1, :] + jnp.where(same_head, upd, 0.0)
        y = jnp.concatenate([jnp.concatenate([t['y'] for t in q['tasks']], axis=1) for q in units], axis=0)
        yc = y - _segsum(y, seg) * (1.0 / RB_N)
        var = _segsum(yc * yc, seg) * (1.0 / RB_N)
        yn = yc * lax.rsqrt(var + RB_GN_EPS) * gnw_ref[...] + gnb_ref[...]
        for i, q in enumerate(units):
            out = (yn[i * G:(i + 1) * G] + q['bonus']) * q['gate']
            if tl >= G:
                o_ref[q['out_idx']] = out
            else:
                o_ref[q['out_idx']] = out.reshape(nb, tl, RB_W)
        return carry

    lax.fori_loop(0, n_iter, step, 0)

    @pl.when(l == pl.num_programs(1) - 1)
    def _fin():
        sh_ref[...] = pr_ref[:, tl - 1:tl, :]
        for p in range(npair):
            S2 = s2_s[:, p]
            s_ref[:, 2 * p] = S2[:, :RB_N, :RB_N]
            s_ref[:, 2 * p + 1] = S2[:, RB_N:, RB_N:]


def _rwkv(pr, shift0, s0, mu, w0, a0, wwa, g2, k_k, k_a, r_k, gn_w, gn_b, bb, tl, C):
    B, L, _ = pr.shape
    const = lambda b, l: (0, 0)
    tok = lambda b, l: (b, l, 0)
    st = lambda b, l: (b, 0, 0, 0)
    vec = lambda x: x.reshape(1, -1)
    kern = functools.partial(_rwkv_kernel, bb=bb, tl=tl, C=C)
    return pl.pallas_call(
        kern,
        out_shape=(jax.ShapeDtypeStruct((B, L, RB_W), F32),
                   jax.ShapeDtypeStruct(s0.shape, F32),
                   jax.ShapeDtypeStruct((B, 1, RB_PROJ_W), F32)),
        grid=(B // bb, L // tl),
        in_specs=[pl.BlockSpec((bb, tl, RB_PROJ_W), tok),
                  pl.BlockSpec((bb, 8, RB_PROJ_W), lambda b, l: (b, jnp.maximum(l * (tl // 8) - 1, 0), 0)),
                  pl.BlockSpec((bb, 1, RB_PROJ_W), lambda b, l: (b, 0, 0)),
                  pl.BlockSpec((bb, RB_HEADS, RB_N, RB_N), st),
                  pl.BlockSpec((1, RB_PROJ_W), const),
                  pl.BlockSpec((1, RB_W), const),
                  pl.BlockSpec((1, RB_W), const),
                  pl.BlockSpec((RB_LORA_WA, 2 * RB_W), const),
                  pl.BlockSpec((RB_LORA_G, RB_W), const),
                  pl.BlockSpec((1, RB_W), const),
                  pl.BlockSpec((1, RB_W), const),
                  pl.BlockSpec((1, RB_W), const),
                  pl.BlockSpec((1, RB_W), const),
                  pl.BlockSpec((1, RB_W), const)],
        out_specs=(pl.BlockSpec((bb, tl, RB_W), tok),
                   pl.BlockSpec((bb, RB_HEADS, RB_N, RB_N), st),
                   pl.BlockSpec((bb, 1, RB_PROJ_W), lambda b, l: (b, 0, 0))),
        scratch_shapes=[pltpu.VMEM((bb, RB_HEADS // 2, 2 * RB_N, 2 * RB_N), F32)],
        compiler_params=pltpu.CompilerParams(dimension_semantics=("parallel", "arbitrary"),
                                             vmem_limit_bytes=VMEM_LIMIT_BYTES),
        name="rwkv",
    )(pr, pr, shift0.reshape(B, 1, RB_PROJ_W), s0, vec(mu), vec(w0), vec(a0), wwa, g2, vec(k_k), vec(k_a),
      vec(r_k), vec(gn_w), vec(gn_b))


def _mixffn_kernel(oa_ref, ob_ref, pg_ref, x_ref, mod_ref, nf_ref, nfin_ref, wua_ref, wub_ref, wo_ref, wfi_ref,
                   wfo_ref, y_ref, *rest, ns):
    bb, tl, d = x_ref.shape
    if rest:
        wbf, (stage, sem) = rest[:5], rest[5:]

        @pl.when((pl.program_id(0) == 0) & (pl.program_id(1) == 0))
        def _cast():
            _cast_weights(list(zip((wua_ref, wub_ref, wo_ref, wfi_ref, wfo_ref), wbf)), stage, sem)
        wua_ref, wub_ref, wo_ref, wfi_ref, wfo_ref = wbf

    def part(bsl, tsl):
        rows = lambda ref: ref[bsl, tsl, :]
        x = rows(x_ref)
        nb_, nt_ = x.shape[0], x.shape[1]
        R = nb_ * nt_
        mod = lambda i: mod_ref[bsl, i:i + 1, :]
        pg = rows(pg_ref).reshape(R, GATE_W)
        ua = _dot(rows(oa_ref).reshape(R, HA_W), wua_ref[...])
        ub = _dot(rows(ob_ref).reshape(R, RB_W), wub_ref[...])
        yield
        merged = _sigmoid(pg[:, :d]) * ua + _sigmoid(pg[:, d:]) * ub
        mix = _dot(merged, wo_ref[...]).reshape(nb_, nt_, d)
        yield
        x1 = x + mod(2) * mix
        xn = x1 * lax.rsqrt(jnp.mean(x1 * x1, axis=-1, keepdims=True) + RMS_EPS) * nf_ref[...]
        xf = (xn * (1.0 + mod(4)) + mod(3)).reshape(R, d)
        h = _dot(xf, wfi_ref[...])
        yield
        gate, up = h[:, :FFN_HIDDEN], h[:, FFN_HIDDEN:]
        act = gate * _sigmoid(gate) * up
        f = _dot(act, wfo_ref[...]).reshape(nb_, nt_, d)
        yield
        x2 = x1 + mod(5) * f
        y_ref[bsl, tsl, :] = x2 * lax.rsqrt(jnp.mean(x2 * x2, axis=-1, keepdims=True) + RMS_EPS) * nfin_ref[...]

    _run_staged(_row_parts(part, bb, tl, ns))


def _mixffn(oa, ob, pg, x, mod, mod_off, norm_ffn, norm_final, weights, bb, tl, ns):
    B, L, d = x.shape
    mod_blk = mod_off // bb
    const = lambda b, l: (0, 0)
    tok = lambda b, l: (b, l, 0)
    cast = weights[0].dtype == F32
    out_shape = [jax.ShapeDtypeStruct((B, L, d), F32)]
    out_specs = [pl.BlockSpec((bb, tl, d), tok)]
    if cast:
        w_specs = [pl.BlockSpec(memory_space=pl.ANY) for _ in weights]
        out_shape += [jax.ShapeDtypeStruct(w.shape, BF16) for w in weights]
        out_specs += [pl.BlockSpec(w.shape, const, pipeline_mode=pl.Buffered(1)) for w in weights]
        scratch = [pltpu.VMEM(STAGE_SHAPE, F32), pltpu.SemaphoreType.DMA((STAGE_SHAPE[0],))]
    else:
        w_specs = [pl.BlockSpec(w.shape, const, pipeline_mode=pl.Buffered(1)) for w in weights]
        scratch = []
    return pl.pallas_call(
        functools.partial(_mixffn_kernel, ns=ns),
        out_shape=tuple(out_shape),
        grid=(B // bb, L // tl),
        in_specs=[pl.BlockSpec((bb, tl, HA_W), tok),
                  pl.BlockSpec((bb, tl, RB_W), tok),
                  pl.BlockSpec((bb, tl, GATE_W), tok),
                  pl.BlockSpec((bb, tl, d), tok),
                  pl.BlockSpec((bb, 6, d), lambda b, l: (b + mod_blk, 0, 0)),
                  pl.BlockSpec((1, d), const),
                  pl.BlockSpec((1, d), const)] + w_specs,
        out_specs=tuple(out_specs),
        scratch_shapes=scratch,
        compiler_params=pltpu.CompilerParams(dimension_semantics=("arbitrary", "arbitrary"),
                                             vmem_limit_bytes=VMEM_LIMIT_BYTES),
        name="mixffn",
    )(oa, ob, pg, x, mod, norm_ffn.reshape(1, d), norm_final.reshape(1, d), *weights)


def _block(x, mod, mod_off, s_h, s_r, shift, p, w_in, w_mix, layer, tiles):
    assert mod_off % tiles['inproj'][0] == 0 and mod_off % tiles['mixffn'][0] == 0
    ph, pr, pg, *w_in_bf = _inproj(x, mod, mod_off, p['norm_mix'], w_in, *tiles['inproj'])
    oa, s_h = _hgrn(ph, s_h, p['lower_bounds'], p['hgrn_norm'], layer, *tiles['hgrn'])
    ob, s_r, shift = _rwkv(pr, shift, s_r, p['mu'], p['w0'], p['a0'], p['wwa'], p['g2'], p['k_k'], p['k_a'], p['r_k'],
                           p['gn_w'], p['gn_b'], *tiles['rwkv'])
    y, *w_mix_bf = _mixffn(oa, ob, pg, x, mod, mod_off, p['norm_ffn'], p['norm_final'], w_mix, *tiles['mixffn'])
    return (y, s_h, s_r, shift.reshape(shift.shape[0], RB_PROJ_W)), (w_in_bf[0] if w_in_bf else w_in), (w_mix_bf or w_mix)


PROMPT_TILES = {'inproj': (1, 512, 4), 'mixffn': (1, 512, 2), 'hgrn': (8, 128, 64, 8), 'rwkv': (4, 128, 64)}
SAMPLE_TILES = {'inproj': (64, 8, 2), 'mixffn': (64, 8, 2), 'hgrn': (16, 8, 8, 8), 'rwkv': (32, 8, 8)}


def kernel(x_prompt, x_sample, c_prompt, c_sample, state_hgrn, state_rwkv, state_shift, w_ada, b_ada, norm_mix, norm_ffn, w_in, hgrn_lower_bounds, hgrn_norm, rwkv_mu, rwkv_w0, rwkv_w2, rwkv_a0, rwkv_a2, rwkv_g2, rwkv_k_k, rwkv_k_a, rwkv_r_k, rwkv_gn_w, rwkv_gn_b, w_up_a, w_up_b, w_out, w_ffn_in, w_ffn_out, norm_final):
    depth = w_ada.shape[0]
    assert depth == 1, "the final norm is fused into the layer kernel"
    nbp, nbs = x_prompt.shape[0], x_sample.shape[0]

    hp = jnp.zeros((nbp, HA_HEADS, HA_DK, HA_DK), F32)
    rp = jnp.zeros((nbp, RB_HEADS, RB_N, RB_N), F32)
    sp = jnp.zeros((nbp, RB_PROJ_W), x_prompt.dtype)

    yp, ys = x_prompt, x_sample
    outs = [[] for _ in range(6)]
    for l in range(depth):
        half = RB_LORA_WA // 2
        zero = jnp.zeros((half, RB_W), F32)
        wwa = jnp.concatenate([jnp.concatenate([rwkv_w2[l], zero], axis=1),
                               jnp.concatenate([zero, rwkv_a2[l]], axis=1)], axis=0)
        p = {
            'norm_mix': norm_mix[l], 'norm_ffn': norm_ffn[l], 'norm_final': norm_final,
            'lower_bounds': hgrn_lower_bounds, 'hgrn_norm': hgrn_norm[l],
            'mu': rwkv_mu[l], 'w0': rwkv_w0[l], 'a0': rwkv_a0[l], 'wwa': wwa.astype(BF16),
            'g2': rwkv_g2[l].astype(BF16), 'k_k': rwkv_k_k[l], 'k_a': rwkv_k_a[l], 'r_k': rwkv_r_k[l],
            'gn_w': rwkv_gn_w[l], 'gn_b': rwkv_gn_b[l],
        }
        w_mix = (w_up_a[l], w_up_b[l], w_out[l], w_ffn_in[l], w_ffn_out[l])
        mod = _ada(jnp.concatenate([c_sample, c_prompt], axis=0), w_ada[l], b_ada[l])
        (yp, h1, r1, s1), w_in_bf, w_mix_bf = _block(yp, mod, nbs, hp, rp, sp, p, w_in[l], w_mix, l, PROMPT_TILES)
        (ys, h2, r2, s2), _, _ = _block(ys, mod, 0, state_hgrn[l], state_rwkv[l], state_shift[l], p, w_in_bf,
                                        tuple(w_mix_bf), l, SAMPLE_TILES)
        for lst, val in zip(outs, (h1, r1, s1, h2, r2, s2)):
            lst.append(val)
    stacked = [jnp.stack(o) for o in outs]
    return (yp, ys, *stacked)
```

```python
import functools
import math

import jax
import jax.numpy as jnp
from jax import lax
from jax.experimental import pallas as pl
from jax.experimental.pallas import tpu as pltpu

F32 = jnp.float32
BF16 = jnp.bfloat16

D_MODEL = 1024
HA_HEADS = 4
HA_DK = 128
HA_W = HA_HEADS * HA_DK
RB_HEADS = 8
RB_N = 64
RB_W = RB_HEADS * RB_N
RB_LORA_WA = 128
RB_LORA_G = 128
RB_PROJ_W = 3 * RB_W + RB_LORA_WA + RB_LORA_G
RB_DECAY_SCALE = math.exp(-0.5)
RB_GN_EPS = 64e-5
HG_W = 4 * HA_W
GATE_W = 2 * D_MODEL
PROJ_W = HG_W + RB_PROJ_W + GATE_W
FFN_HIDDEN = 2816
RMS_EPS = 1e-6
LOG2_E = math.log2(math.e)
GATE_DTYPE = BF16

MXU_W = 256
SUBLANES = 8
RB_KK_EPS = 1e-12
VMEM_LIMIT_BYTES = 58 * 1024 * 1024
STAGE_SHAPE = (8, 1024, 128)
UNITS_PER_ITER = 4
RWKV_GROUP = 64


def _dot(a, b):
    return jnp.dot(a.astype(BF16), b.astype(BF16), preferred_element_type=F32)


def _dot_nt(a, b):
    return lax.dot_general(a.astype(BF16), b.astype(BF16), (((1,), (1,)), ((), ())), preferred_element_type=F32)


def _dot_tn(a, b):
    return lax.dot_general(a.astype(BF16), b.astype(BF16), (((0,), (0,)), ((), ())), preferred_element_type=F32)


def _sigmoid(x):
    return 0.5 * jnp.tanh(0.5 * x) + 0.5


def _segsum(x, seg):
    w = seg.shape[0]
    xb = x.astype(BF16)
    return jnp.concatenate([jnp.dot(xb[:, i:i + w], seg, preferred_element_type=F32)
                            for i in range(0, x.shape[1], w)], axis=1)


def _sum3(m, x):
    hi = x.astype(BF16)
    r1 = x - hi.astype(F32)
    mid = r1.astype(BF16)
    lo = (r1 - mid.astype(F32)).astype(BF16)
    return (jnp.dot(m, hi, preferred_element_type=F32) + jnp.dot(m, mid, preferred_element_type=F32)
            + jnp.dot(m, lo, preferred_element_type=F32))


def _run_staged(gens):
    gens = list(gens)
    while gens:
        for g in list(gens):
            try:
                next(g)
            except StopIteration:
                gens.remove(g)


def _row_parts(part, bb, tl, ns):
    if bb == 1:
        step = tl // ns
        return [part(slice(None), slice(i * step, (i + 1) * step)) for i in range(ns)]
    step = bb // ns
    return [part(slice(i * step, (i + 1) * step), slice(None)) for i in range(ns)]


def _cast_weights(pairs, stage, sem):
    nslot, srows, scols = stage.shape
    chunks = []
    for src, dst in pairs:
        nrow, ncol = src.shape
        for r0 in range(0, nrow, srows):
            for c0 in range(0, ncol, scols):
                chunks.append((src, dst, r0, min(srows, nrow - r0), c0, min(scols, ncol - c0)))

    def copy(i):
        src, _, r0, nr, c0, nc = chunks[i]
        return pltpu.make_async_copy(src.at[pl.ds(r0, nr), pl.ds(c0, nc)],
                                     stage.at[i % nslot, pl.ds(0, nr), pl.ds(0, nc)], sem.at[i % nslot])

    for i in range(min(nslot - 1, len(chunks))):
        copy(i).start()
    for i, (_, dst, r0, nr, c0, nc) in enumerate(chunks):
        if i + nslot - 1 < len(chunks):
            copy(i + nslot - 1).start()
        copy(i).wait()
        dst[r0:r0 + nr, c0:c0 + nc] = stage[i % nslot, 0:nr, 0:nc].astype(BF16)


def _iota(shape, axis):
    return lax.broadcasted_iota(jnp.int32, shape, axis)


def _ada_kernel(c_ref, w_ref, b_ref, o_ref, *, per_step):
    c = c_ref[...]
    d = c.shape[1]
    val = _dot(c * _sigmoid(c), w_ref[...]) + b_ref[...]
    for j in range(o_ref.shape[1] // per_step):
        @pl.when(pl.program_id(0) == j)
        def _store():
            for i in range(per_step):
                o_ref[:, j * per_step + i, :] = val[:, i * d:(i + 1) * d]


def _ada(c_all, w_ada, b_ada):
    n, d = c_all.shape
    nmod = w_ada.shape[1] // d
    per_step = 2
    return pl.pallas_call(
        functools.partial(_ada_kernel, per_step=per_step),
        out_shape=jax.ShapeDtypeStruct((n, nmod, d), F32),
        grid=(nmod // per_step,),
        in_specs=[pl.BlockSpec((n, d), lambda j: (0, 0)),
                  pl.BlockSpec((d, per_step * d), lambda j: (0, j)),
                  pl.BlockSpec((1, per_step * d), lambda j: (0, j))],
        out_specs=pl.BlockSpec((n, nmod, d), lambda j: (0, 0, 0)),
        compiler_params=pltpu.CompilerParams(dimension_semantics=("arbitrary",)),
        name="ada",
    )(c_all, w_ada, b_ada.reshape(1, nmod * d))


def _inproj_kernel(x_ref, mod_ref, g_ref, w_ref, oh_ref, or_ref, og_ref, *rest, ns):
    bb, tl, d = x_ref.shape
    if rest:
        wbf_ref, stage, sem = rest

        @pl.when((pl.program_id(0) == 0) & (pl.program_id(1) == 0))
        def _cast():
            _cast_weights([(w_ref, wbf_ref)], stage, sem)
        w_ref = wbf_ref

    def part(bsl, tsl):
        x = x_ref[bsl, tsl, :]
        nb_, nt_ = x.shape[0], x.shape[1]
        xn = x * lax.rsqrt(jnp.mean(x * x, axis=-1, keepdims=True) + RMS_EPS) * g_ref[...]
        xm = xn * (1.0 + mod_ref[bsl, 1:2, :]) + mod_ref[bsl, 0:1, :]
        xb = xm.reshape(nb_ * nt_, d).astype(BF16)
        yield
        lo = 0
        for o_ref in (oh_ref, or_ref, og_ref):
            w = o_ref.shape[-1]
            res = jnp.dot(xb, w_ref[:, lo:lo + w], preferred_element_type=F32).reshape(nb_, nt_, w)
            o_ref[bsl, tsl, :] = res.astype(o_ref.dtype)
            lo += w
            yield

    _run_staged(_row_parts(part, bb, tl, ns))


def _inproj(x, mod, mod_off, norm_g, w_in, bb, tl, ns):
    B, L, d = x.shape
    mod_blk = mod_off // bb
    const = lambda b, l: (0, 0)
    tok = lambda b, l: (b, l, 0)
    cast = w_in.dtype == F32
    out_shape = [jax.ShapeDtypeStruct((B, L, HG_W), F32),
                 jax.ShapeDtypeStruct((B, L, RB_PROJ_W), F32),
                 jax.ShapeDtypeStruct((B, L, GATE_W), GATE_DTYPE)]
    out_specs = [pl.BlockSpec((bb, tl, HG_W), tok),
                 pl.BlockSpec((bb, tl, RB_PROJ_W), tok),
                 pl.BlockSpec((bb, tl, GATE_W), tok)]
    if cast:
        w_spec = pl.BlockSpec(memory_space=pl.ANY)
        out_shape.append(jax.ShapeDtypeStruct(w_in.shape, BF16))
        out_specs.append(pl.BlockSpec(w_in.shape, const, pipeline_mode=pl.Buffered(1)))
        scratch = [pltpu.VMEM(STAGE_SHAPE, F32), pltpu.SemaphoreType.DMA((STAGE_SHAPE[0],))]
    else:
        w_spec = pl.BlockSpec((d, PROJ_W), const, pipeline_mode=pl.Buffered(1))
        scratch = []
    return pl.pallas_call(
        functools.partial(_inproj_kernel, ns=ns),
        out_shape=tuple(out_shape),
        grid=(B // bb, L // tl),
        in_specs=[pl.BlockSpec((bb, tl, d), tok),
                  pl.BlockSpec((bb, 6, d), lambda b, l: (b + mod_blk, 0, 0)),
                  pl.BlockSpec((1, d), const),
                  w_spec],
        out_specs=tuple(out_specs),
        scratch_shapes=scratch,
        compiler_params=pltpu.CompilerParams(dimension_semantics=("arbitrary", "arbitrary"),
                                             vmem_limit_bytes=VMEM_LIMIT_BYTES),
        name="inproj",
    )(x, mod, norm_g.reshape(1, d), w_in)


def _hgrn_kernel(ph_ref, s0_ref, lbp_ref, ng_ref, o_ref, s_ref, c2_s, *, layer, bb, nch, C, c):
    @pl.when(pl.program_id(1) == 0)
    def _init():
        s_ref[...] = s0_ref[...]

    hb = lbp_ref[...]
    e = jnp.exp(hb - jnp.max(hb, axis=0, keepdims=True))
    lb = jnp.sum(e[0:layer + 1], axis=0, keepdims=True) / jnp.sum(e, axis=0, keepdims=True)
    ng = ng_ref[...]
    cum_m = (_iota((C, C), 0) >= _iota((C, C), 1)).astype(BF16)
    nsub = C // c
    causal = _iota((C, C), 0) >= _iota((C, C), 1)

    def one_chunk(ci):
        r0 = pl.multiple_of(ci * C, C)
        tasks, units = [], []
        for u in range(bb):
            ph = ph_ref[u, pl.ds(r0, C), :]
            hq = ph[:, 0:HA_W]
            hf = ph[:, HA_W:2 * HA_W]
            hi = ph[:, 2 * HA_W:3 * HA_W]
            hg = ph[:, 3 * HA_W:4 * HA_W]
            q = hq * _sigmoid(hq)
            sg = 1.0 / (1.0 + jnp.exp(-hf))
            logf = jnp.log(lb + (1.0 - lb) * sg)
            b2 = _sum3(cum_m, logf) * LOG2_E
            c2 = b2 - jnp.log2((1.0 - lb) * (1.0 - sg))
            c2_s[u] = c2
            unit = dict(u=u, gate=hg * _sigmoid(hg), tasks=[])
            for h in range(HA_HEADS):
                sl = slice(h * HA_DK, (h + 1) * HA_DK)
                t = dict(u=u, h=h, q=q[:, sl], v=hi[:, sl], b2=b2[:, sl], c2=c2[:, sl])
                tasks.append(t)
                unit['tasks'].append(t)
            units.append(unit)
        for t in tasks:
            t['S'] = s_ref[t['u'], t['h']]
        for t in tasks:
            t['o'] = _dot(t['q'] * jnp.exp2(t['b2']), t['S'])
        lane_is = [_iota((c, C), 1) == j for j in range(C)]
        for t in tasks:
            q, b2, c2 = t['q'], t['b2'], t['c2']
            sl = slice(t['h'] * HA_DK, (t['h'] + 1) * HA_DK)
            rows = []
            for i in range(nsub):
                lo = i * c
                qi, bi = q[lo:lo + c], b2[lo:lo + c]
                if i > 0:
                    bref = b2[lo - 1:lo, :]
                    kt = jnp.concatenate([jnp.exp2(bref - c2[:lo]), jnp.zeros((C - lo, HA_DK), F32)], axis=0)
                    sc = _dot_nt(qi * jnp.exp2(bi - bref), kt)
                else:
                    sc = jnp.zeros((c, C), F32)
                for s in range(c):
                    col = jnp.sum(qi * jnp.exp2(bi - c2_s[t['u'], lo + s:lo + s + 1, sl]), axis=-1, keepdims=True)
                    sc = jnp.where(lane_is[lo + s], col, sc)
                rows.append(sc)
            t['scores'] = jnp.where(causal, jnp.concatenate(rows, axis=0), 0.0)
        for t in tasks:
            t['o'] = t['o'] + _dot(t['scores'], t['v'])
        for t in tasks:
            bl = t['b2'][C - 1:C, :]
            kd = jnp.exp2(bl - t['c2'])
            dcol = jnp.transpose(jnp.broadcast_to(jnp.exp2(bl), (SUBLANES, HA_DK)))[:, 0:1]
            s_ref[t['u'], t['h']] = t['S'] * dcol + _dot_tn(kd, t['v'])
        for unit in units:
            outs = [t['o'] * lax.rsqrt(jnp.mean(t['o'] * t['o'], axis=-1, keepdims=True) + RMS_EPS)
                    for t in unit['tasks']]
            o_ref[unit['u'], pl.ds(r0, C), :] = (jnp.concatenate(outs, axis=1) * ng * unit['gate']).astype(o_ref.dtype)

    def per_chunk(ci, carry):
        one_chunk(ci)
        return carry

    lax.fori_loop(0, nch, per_chunk, 0)


def _hgrn(ph, s0, lower_bounds, norm_g, layer, bb, tl, C, c, out_dtype):
    B, L, _ = ph.shape
    const = lambda b, l: (0, 0)
    st = lambda b, l: (b, 0, 0, 0)
    kern = functools.partial(_hgrn_kernel, layer=layer, bb=bb, nch=tl // C, C=C, c=c)
    return pl.pallas_call(
        kern,
        out_shape=(jax.ShapeDtypeStruct((B, L, HA_W), out_dtype),
                   jax.ShapeDtypeStruct(s0.shape, F32)),
        grid=(B // bb, L // tl),
        in_specs=[pl.BlockSpec((bb, tl, HG_W), lambda b, l: (b, l, 0)),
                  pl.BlockSpec((bb, HA_HEADS, HA_DK, HA_DK), st),
                  pl.BlockSpec(lower_bounds.shape, const),
                  pl.BlockSpec((1, HA_W), const)],
        out_specs=(pl.BlockSpec((bb, tl, HA_W), lambda b, l: (b, l, 0)),
                   pl.BlockSpec((bb, HA_HEADS, HA_DK, HA_DK), st)),
        scratch_shapes=[pltpu.VMEM((bb, C, HA_W), F32)],
        compiler_params=pltpu.CompilerParams(dimension_semantics=("parallel", "arbitrary"),
                                             vmem_limit_bytes=VMEM_LIMIT_BYTES),
        name="hgrn",
    )(ph, s0, lower_bounds, norm_g.reshape(1, HA_W))


def _rwkv_kernel(pr_ref, prev_ref, sh0_ref, s0_ref, mu_ref, w0_ref, a0_ref, wwa_ref, g2_ref, kk_ref, ka_ref,
                 rk_ref, gnw_ref, gnb_ref, o_ref, s_ref, sh_ref, s2_s, *, bb, tl, C):
    l = pl.program_id(1)
    G = RWKV_GROUP
    nb = G // C
    npair = RB_HEADS // 2
    W = RB_PROJ_W

    if nb == 1:
        @pl.when(l == 0)
        def _init():
            z = jnp.zeros((bb, RB_N, RB_N), F32)
            for p in range(npair):
                top = jnp.concatenate([s0_ref[:, 2 * p], z], axis=2)
                bot = jnp.concatenate([z, s0_ref[:, 2 * p + 1]], axis=2)
                s2_s[:, p] = jnp.concatenate([top, bot], axis=1)

    rg, cg = _iota((G, G), 0), _iota((G, G), 1)
    same_chunk = (rg // C) == (cg // C)
    cum_m = (same_chunk & (rg >= cg)).astype(BF16)
    if nb > 1:
        cum_m = jnp.concatenate([cum_m, same_chunk.astype(BF16)], axis=0)
    rp, cp = _iota((G, 2 * RB_N), 0), _iota((G, 2 * RB_N), 1) % RB_N
    same_c = (rp // C) == (cp // C)
    tri_s = same_c & ((rp % C) > (cp % C))
    tri_i = same_c & ((rp % C) >= (cp % C))
    head0 = _iota((G, 2 * RB_N), 1) < RB_N
    same_head = (_iota((2 * RB_N, 2 * RB_N), 0) // RB_N) == (_iota((2 * RB_N, 2 * RB_N), 1) // RB_N)
    seg = ((_iota((MXU_W, MXU_W), 0) // RB_N) == (_iota((MXU_W, MXU_W), 1) // RB_N)).astype(BF16)
    eye2 = jnp.where(rp == cp, 1.0, 0.0)
    is_tanh = _iota((G, RB_LORA_WA), 1) < RB_LORA_WA // 2
    nsq = int(math.log2(C)) - 1

    def stack2(x):
        return jnp.concatenate([jnp.where(head0, x, 0.0), jnp.where(head0, 0.0, x)], axis=0)

    if tl >= G:
        n_iter, n_unit = tl // G, bb
    else:
        n_unit = min(UNITS_PER_ITER, bb * tl // G)
        n_iter = bb * tl // (G * n_unit)
    sls = [slice(p * 2 * RB_N, (p + 1) * 2 * RB_N) for p in range(npair)]

    def load_rows(it, u):
        if tl >= G:
            r0 = pl.multiple_of(it * G, G)
            x = pr_ref[u, pl.ds(r0, G), :]
            before = pr_ref[u, pl.ds(jnp.maximum(r0 - 1, 0), 1), :]
            head = jnp.where(l == 0, sh0_ref[u], prev_ref[u, SUBLANES - 1:SUBLANES, :])
            first = jnp.where(it == 0, head, before)
            prev = jnp.where(_iota((G, W), 0) == 0, first, pltpu.roll(x, 1, 0))
            return x, prev, [u], (u, pl.ds(r0, G))
        b0 = pl.multiple_of((it * n_unit + u) * nb, nb)
        x = pr_ref[pl.ds(b0, nb)]
        rolled = pltpu.roll(x.reshape(G, W), 1, 0).reshape(nb, tl, W)
        prev = jnp.where(_iota((nb, tl, W), 1) == 0, sh0_ref[pl.ds(b0, nb)], rolled)
        return x.reshape(G, W), prev.reshape(G, W), [b0 + j for j in range(nb)], (pl.ds(b0, nb),)

    def step(it, carry):
        units = []
        for u in range(n_unit):
            x, xprev, bs, out_idx = load_rows(it, u)
            pm = x + (xprev - x) * mu_ref[...]
            dwa = pm[:, 3 * RB_W:3 * RB_W + RB_LORA_WA]
            units.append(dict(bs=bs, out_idx=out_idx, r=pm[:, 0:RB_W], k=pm[:, RB_W:2 * RB_W],
                              v=pm[:, 2 * RB_W:3 * RB_W], dwa=jnp.where(is_tanh, jnp.tanh(dwa), dwa),
                              sdg=_sigmoid(pm[:, 3 * RB_W + RB_LORA_WA:])))
        wa = _dot(jnp.concatenate([q['dwa'] for q in units], axis=0), wwa_ref[...])
        gate = _dot(jnp.concatenate([q['sdg'] for q in units], axis=0), g2_ref[...])
        for i, q in enumerate(units):
            rs = slice(i * G, (i + 1) * G)
            q['lw'] = -RB_DECAY_SCALE * _sigmoid(w0_ref[...] + wa[rs, 0:RB_W])
            q['a'] = _sigmoid(a0_ref[...] + wa[rs, RB_W:])
            q['gate'] = gate[rs]
            q['kk'] = q['k'] * kk_ref[...]
            q['kp'] = q['k'] * (1.0 + (q['a'] - 1.0) * ka_ref[...])
        sums = _segsum(jnp.concatenate([q['kk'] * q['kk'] for q in units]
                                       + [q['r'] * q['kp'] * rk_ref[...] for q in units], axis=0), seg)
        tasks = []
        for i, q in enumerate(units):
            kk = q['kk'] * lax.rsqrt(jnp.maximum(sums[i * G:(i + 1) * G], RB_KK_EPS * RB_KK_EPS))
            q['bonus'] = sums[(n_unit + i) * G:(n_unit + i + 1) * G] * q['v']
            kb = kk * q['a']
            lw, kp = q['lw'], q['kp']
            gs = _sum3(cum_m, lw)
            g = gs[:G]
            gl = gs[G:] if nb > 1 else g[G - 1:G, :]
            eg, eng, egl = jnp.exp(g), jnp.exp(-g), jnp.exp(gl)
            At = -kk * (eg * jnp.exp(-lw))
            Bt = kb * eng
            Kt = kp * eng
            Rt = q['r'] * eg
            dec = egl * eng
            Ke = kp * dec
            Be = kb * dec
            q['tasks'] = []
            for p in range(npair):
                sl = sls[p]
                t = dict(p=p, bs=q['bs'], AR=jnp.concatenate([At[:, sl], Rt[:, sl]], axis=0), B=Bt[:, sl],
                         K=Kt[:, sl], V=q['v'][:, sl], Ke=Ke[:, sl], Be=Be[:, sl], egl=egl[:, sl])
                tasks.append(t)
                q['tasks'].append(t)
        for t in tasks:
            if nb == 1:
                t['S2'] = [s2_s[b, t['p']] for b in t['bs']]
            else:
                t['S'] = [[s0_ref[b, 2 * t['p'] + h] for h in range(2)] for b in t['bs']]
        for t in tasks:
            t['M'] = _dot_nt(t['AR'], jnp.concatenate([stack2(t['B']), stack2(t['K'])], axis=0))
        for t in tasks:
            if nb == 1:
                x0 = _dot_nt(t['AR'], t['S2'][0])
                t['x0a'], t['x0r'] = x0[:G], x0[G:]
            else:
                x0 = []
                for j in range(nb):
                    arj = jnp.concatenate([t['AR'][j * C:(j + 1) * C], t['AR'][G + j * C:G + (j + 1) * C]], axis=0)
                    x0.append(jnp.concatenate([_dot_nt(arj[:, h * RB_N:(h + 1) * RB_N], t['S'][j][h])
                                               for h in range(2)], axis=1))
                t['x0a'] = jnp.concatenate([x[:C] for x in x0], axis=0)
                t['x0r'] = jnp.concatenate([x[C:] for x in x0], axis=0)
        for t in tasks:
            M = t['M']
            lak = jnp.where(tri_s, M[:G, 2 * RB_N:], 0.0)
            lrk = jnp.where(tri_i, M[G:, 2 * RB_N:], 0.0)
            lv = _dot(jnp.concatenate([lak, lrk], axis=0), stack2(t['V']))
            t['X'] = t['x0a'] + lv[:G]
            t['Yv'] = t['x0r'] + lv[G:]
            t['P'] = jnp.where(tri_s, M[:G, :2 * RB_N], 0.0)
        for t in tasks:
            t['T'] = eye2 + t['P']
        for k in range(nsq):
            for t in tasks:
                rhs = stack2(t['P'])
                if k == 0:
                    t['P'] = _dot(t['P'], rhs)
                else:
                    pt = _dot(jnp.concatenate([t['P'], t['T']], axis=0), rhs)
                    t['P'] = pt[:G]
                    t['T'] = t['T'] + pt[G:]
        for t in tasks:
            t['T'] = t['T'] + _dot(t['T'], stack2(t['P']))
        for t in tasks:
            t['SA'] = _dot(t['T'], stack2(t['X']))
        for t in tasks:
            lrb = jnp.where(tri_i, t['M'][G:, :2 * RB_N], 0.0)
            t['y'] = t['Yv'] + _dot(lrb, stack2(t['SA']))
        for t in tasks:
            for j in range(nb):
                js = slice(j * C, (j + 1) * C)
                lhs = jnp.concatenate([t['V'][js], t['SA'][js]], axis=0)
                rhs = jnp.concatenate([t['Ke'][js], t['Be'][js]], axis=0)
                egl_j = t['egl'][j * C:j * C + 1, :]
                if nb == 1:
                    s2_s[t['bs'][j], t['p']] = t['S2'][j] * egl_j + jnp.where(same_head, _dot_tn(lhs, rhs), 0.0)
                else:
                    for h in range(2):
                        hs = slice(h * RB_N, (h + 1) * RB_N)
                        s_ref[t['bs'][j], 2 * t['p'] + h] = t['S'][j][h] * egl_j[:, hs] + _dot_tn(lhs[:, hs], rhs[:, hs])
        y = jnp.concatenate([jnp.concatenate([t['y'] for t in q['tasks']], axis=1) for q in units], axis=0)
        yc = y - _segsum(y, seg) * (1.0 / RB_N)
        var = _segsum(yc * yc, seg) * (1.0 / RB_N)
        yn = yc * lax.rsqrt(var + RB_GN_EPS) * gnw_ref[...] + gnb_ref[...]
        for i, q in enumerate(units):
            out = ((yn[i * G:(i + 1) * G] + q['bonus']) * q['gate']).astype(o_ref.dtype)
            if tl >= G:
                o_ref[q['out_idx']] = out
            else:
                o_ref[q['out_idx']] = out.reshape(nb, tl, RB_W)
        return carry

    lax.fori_loop(0, n_iter, step, 0)

    @pl.when(l == pl.num_programs(1) - 1)
    def _fin():
        sh_ref[...] = pr_ref[:, tl - 1:tl, :]
        if nb == 1:
            for p in range(npair):
                S2 = s2_s[:, p]
                s_ref[:, 2 * p] = S2[:, :RB_N, :RB_N]
                s_ref[:, 2 * p + 1] = S2[:, RB_N:, RB_N:]


def _rwkv(pr, shift0, s0, mu, w0, a0, wwa, g2, k_k, k_a, r_k, gn_w, gn_b, bb, tl, C, out_dtype):
    B, L, _ = pr.shape
    const = lambda b, l: (0, 0)
    tok = lambda b, l: (b, l, 0)
    st = lambda b, l: (b, 0, 0, 0)
    vec = lambda x: x.reshape(1, -1)
    kern = functools.partial(_rwkv_kernel, bb=bb, tl=tl, C=C)
    assert C == RWKV_GROUP or (tl == C and L == tl)
    pair_rows = bb if C == RWKV_GROUP else 1
    return pl.pallas_call(
        kern,
        out_shape=(jax.ShapeDtypeStruct((B, L, RB_W), out_dtype),
                   jax.ShapeDtypeStruct(s0.shape, F32),
                   jax.ShapeDtypeStruct((B, 1, RB_PROJ_W), F32)),
        grid=(B // bb, L // tl),
        in_specs=[pl.BlockSpec((bb, tl, RB_PROJ_W), tok),
                  pl.BlockSpec((bb, SUBLANES, RB_PROJ_W), lambda b, l: (b, jnp.maximum(l * (tl // SUBLANES) - 1, 0), 0)),
                  pl.BlockSpec((bb, 1, RB_PROJ_W), lambda b, l: (b, 0, 0)),
                  pl.BlockSpec((bb, RB_HEADS, RB_N, RB_N), st),
                  pl.BlockSpec((1, RB_PROJ_W), const),
                  pl.BlockSpec((1, RB_W), const),
                  pl.BlockSpec((1, RB_W), const),
                  pl.BlockSpec((RB_LORA_WA, 2 * RB_W), const),
                  pl.BlockSpec((RB_LORA_G, RB_W), const),
                  pl.BlockSpec((1, RB_W), const),
                  pl.BlockSpec((1, RB_W), const),
                  pl.BlockSpec((1, RB_W), const),
                  pl.BlockSpec((1, RB_W), const),
                  pl.BlockSpec((1, RB_W), const)],
        out_specs=(pl.BlockSpec((bb, tl, RB_W), tok),
                   pl.BlockSpec((bb, RB_HEADS, RB_N, RB_N), st),
                   pl.BlockSpec((bb, 1, RB_PROJ_W), lambda b, l: (b, 0, 0))),
        scratch_shapes=[pltpu.VMEM((pair_rows, RB_HEADS // 2, 2 * RB_N, 2 * RB_N), F32)],
        compiler_params=pltpu.CompilerParams(dimension_semantics=("parallel", "arbitrary"),
                                             vmem_limit_bytes=VMEM_LIMIT_BYTES),
        name="rwkv",
    )(pr, pr, shift0.reshape(B, 1, RB_PROJ_W), s0, vec(mu), vec(w0), vec(a0), wwa, g2, vec(k_k), vec(k_a),
      vec(r_k), vec(gn_w), vec(gn_b))


def _mixffn_kernel(oa_ref, ob_ref, pg_ref, x_ref, mod_ref, nf_ref, nfin_ref, wua_ref, wub_ref, wo_ref, wfi_ref,
                   wfo_ref, y_ref, *rest, ns):
    bb, tl, d = x_ref.shape
    if rest:
        wbf, (stage, sem) = rest[:5], rest[5:]

        @pl.when((pl.program_id(0) == 0) & (pl.program_id(1) == 0))
        def _cast():
            _cast_weights(list(zip((wua_ref, wub_ref, wo_ref, wfi_ref, wfo_ref), wbf)), stage, sem)
        wua_ref, wub_ref, wo_ref, wfi_ref, wfo_ref = wbf

    def part(bsl, tsl):
        rows = lambda ref: ref[bsl, tsl, :]
        x = rows(x_ref)
        nb_, nt_ = x.shape[0], x.shape[1]
        R = nb_ * nt_
        mod = lambda i: mod_ref[bsl, i:i + 1, :]
        pg = rows(pg_ref).astype(F32).reshape(R, GATE_W)
        ua = _dot(rows(oa_ref).reshape(R, HA_W), wua_ref[...])
        ub = _dot(rows(ob_ref).reshape(R, RB_W), wub_ref[...])
        yield
        merged = _sigmoid(pg[:, :d]) * ua + _sigmoid(pg[:, d:]) * ub
        mix = _dot(merged, wo_ref[...]).reshape(nb_, nt_, d)
        yield
        x1 = x + mod(2) * mix
        xn = x1 * lax.rsqrt(jnp.mean(x1 * x1, axis=-1, keepdims=True) + RMS_EPS) * nf_ref[...]
        xf = (xn * (1.0 + mod(4)) + mod(3)).reshape(R, d)
        h = _dot(xf, wfi_ref[...])
        yield
        gate, up = h[:, :FFN_HIDDEN], h[:, FFN_HIDDEN:]
        act = gate * _sigmoid(gate) * up
        f = _dot(act, wfo_ref[...]).reshape(nb_, nt_, d)
        yield
        x2 = x1 + mod(5) * f
        y_ref[bsl, tsl, :] = x2 * lax.rsqrt(jnp.mean(x2 * x2, axis=-1, keepdims=True) + RMS_EPS) * nfin_ref[...]

    _run_staged(_row_parts(part, bb, tl, ns))


def _mixffn(oa, ob, pg, x, mod, mod_off, norm_ffn, norm_final, weights, bb, tl, ns):
    B, L, d = x.shape
    mod_blk = mod_off // bb
    const = lambda b, l: (0, 0)
    tok = lambda b, l: (b, l, 0)
    cast = weights[0].dtype == F32
    out_shape = [jax.ShapeDtypeStruct((B, L, d), F32)]
    out_specs = [pl.BlockSpec((bb, tl, d), tok)]
    if cast:
        w_specs = [pl.BlockSpec(memory_space=pl.ANY) for _ in weights]
        out_shape += [jax.ShapeDtypeStruct(w.shape, BF16) for w in weights]
        out_specs += [pl.BlockSpec(w.shape, const, pipeline_mode=pl.Buffered(1)) for w in weights]
        scratch = [pltpu.VMEM(STAGE_SHAPE, F32), pltpu.SemaphoreType.DMA((STAGE_SHAPE[0],))]
    else:
        w_specs = [pl.BlockSpec(w.shape, const, pipeline_mode=pl.Buffered(1)) for w in weights]
        scratch = []
    return pl.pallas_call(
        functools.partial(_mixffn_kernel, ns=ns),
        out_shape=tuple(out_shape),
        grid=(B // bb, L // tl),
        in_specs=[pl.BlockSpec((bb, tl, HA_W), tok),
                  pl.BlockSpec((bb, tl, RB_W), tok),
                  pl.BlockSpec((bb, tl, GATE_W), tok),
                  pl.BlockSpec((bb, tl, d), tok),
                  pl.BlockSpec((bb, 6, d), lambda b, l: (b + mod_blk, 0, 0)),
                  pl.BlockSpec((1, d), const),
                  pl.BlockSpec((1, d), const)] + w_specs,
        out_specs=tuple(out_specs),
        scratch_shapes=scratch,
        compiler_params=pltpu.CompilerParams(dimension_semantics=("arbitrary", "arbitrary"),
                                             vmem_limit_bytes=VMEM_LIMIT_BYTES),
        name="mixffn",
    )(oa, ob, pg, x, mod, norm_ffn.reshape(1, d), norm_final.reshape(1, d), *weights)


def _block(x, mod, mod_off, s_h, s_r, shift, p, w_in, w_mix, layer, tiles):
    assert mod_off % tiles['inproj'][0] == 0 and mod_off % tiles['mixffn'][0] == 0
    ph, pr, pg, *w_in_bf = _inproj(x, mod, mod_off, p['norm_mix'], w_in, *tiles['inproj'])
    oa, s_h = _hgrn(ph, s_h, p['lower_bounds'], p['hgrn_norm'], layer, *tiles['hgrn'])
    ob, s_r, shift = _rwkv(pr, shift, s_r, p['mu'], p['w0'], p['a0'], p['wwa'], p['g2'], p['k_k'], p['k_a'], p['r_k'],
                           p['gn_w'], p['gn_b'], *tiles['rwkv'])
    y, *w_mix_bf = _mixffn(oa, ob, pg, x, mod, mod_off, p['norm_ffn'], p['norm_final'], w_mix, *tiles['mixffn'])
    return (y, s_h, s_r, shift.reshape(shift.shape[0], RB_PROJ_W)), (w_in_bf[0] if w_in_bf else w_in), (w_mix_bf or w_mix)


PROMPT_TILES = {'inproj': (1, 512, 4), 'mixffn': (1, 512, 2), 'hgrn': (8, 128, 64, 8, BF16), 'rwkv': (4, 256, 64, BF16)}
SAMPLE_TILES = {'inproj': (64, 8, 2), 'mixffn': (64, 8, 2), 'hgrn': (16, 8, 8, 8, F32), 'rwkv': (32, 8, 8, F32)}


def kernel(x_prompt, x_sample, c_prompt, c_sample, state_hgrn, state_rwkv, state_shift, w_ada, b_ada, norm_mix, norm_ffn, w_in, hgrn_lower_bounds, hgrn_norm, rwkv_mu, rwkv_w0, rwkv_w2, rwkv_a0, rwkv_a2, rwkv_g2, rwkv_k_k, rwkv_k_a, rwkv_r_k, rwkv_gn_w, rwkv_gn_b, w_up_a, w_up_b, w_out, w_ffn_in, w_ffn_out, norm_final):
    depth = w_ada.shape[0]
    assert depth == 1, "the final norm is fused into the layer kernel"
    nbp, nbs = x_prompt.shape[0], x_sample.shape[0]

    hp = jnp.zeros((nbp, HA_HEADS, HA_DK, HA_DK), F32)
    rp = jnp.zeros((nbp, RB_HEADS, RB_N, RB_N), F32)
    sp = jnp.zeros((nbp, RB_PROJ_W), x_prompt.dtype)

    yp, ys = x_prompt, x_sample
    outs = [[] for _ in range(6)]
    for l in range(depth):
        half = RB_LORA_WA // 2
        zero = jnp.zeros((half, RB_W), F32)
        wwa = jnp.concatenate([jnp.concatenate([rwkv_w2[l], zero], axis=1),
                               jnp.concatenate([zero, rwkv_a2[l]], axis=1)], axis=0)
        p = {
            'norm_mix': norm_mix[l], 'norm_ffn': norm_ffn[l], 'norm_final': norm_final,
            'lower_bounds': hgrn_lower_bounds, 'hgrn_norm': hgrn_norm[l],
            'mu': rwkv_mu[l], 'w0': rwkv_w0[l], 'a0': rwkv_a0[l], 'wwa': wwa.astype(BF16),
            'g2': rwkv_g2[l].astype(BF16), 'k_k': rwkv_k_k[l], 'k_a': rwkv_k_a[l], 'r_k': rwkv_r_k[l],
            'gn_w': rwkv_gn_w[l], 'gn_b': rwkv_gn_b[l],
        }
        w_mix = (w_up_a[l], w_up_b[l], w_out[l], w_ffn_in[l], w_ffn_out[l])
        mod = _ada(jnp.concatenate([c_sample, c_prompt], axis=0), w_ada[l], b_ada[l])
        (yp, h1, r1, s1), w_in_bf, w_mix_bf = _block(yp, mod, nbs, hp, rp, sp, p, w_in[l], w_mix, l, PROMPT_TILES)
        (ys, h2, r2, s2), _, _ = _block(ys, mod, 0, state_hgrn[l], state_rwkv[l], state_shift[l], p, w_in_bf,
                                        tuple(w_mix_bf), l, SAMPLE_TILES)
        for lst, val in zip(outs, (h1, r1, s1, h2, r2, s2)):
            lst.append(val)
    stacked = [jnp.stack(o) for o in outs]
    return (yp, ys, *stacked)
```

```python
import functools
import math

import jax
import jax.numpy as jnp
from jax import lax
from jax.experimental import pallas as pl
from jax.experimental.pallas import tpu as pltpu

F32 = jnp.float32
BF16 = jnp.bfloat16

D_MODEL = 1024
HA_HEADS = 4
HA_DK = 128
HA_W = HA_HEADS * HA_DK
RB_HEADS = 8
RB_N = 64
RB_W = RB_HEADS * RB_N
RB_LORA_WA = 128
RB_LORA_G = 128
RB_PROJ_W = 3 * RB_W + RB_LORA_WA + RB_LORA_G
RB_DECAY_SCALE = math.exp(-0.5)
RB_GN_EPS = 64e-5
HG_W = 4 * HA_W
GATE_W = 2 * D_MODEL
PROJ_W = HG_W + RB_PROJ_W + GATE_W
FFN_HIDDEN = 2816
RMS_EPS = 1e-6
LOG2_E = math.log2(math.e)
GATE_DTYPE = BF16

MXU_W = 256
FFN_GROUPS = ((0, 6 * MXU_W), (6 * MXU_W, FFN_HIDDEN))
SUBLANES = 8
RB_KK_EPS = 1e-12
VMEM_LIMIT_BYTES = 58 * 1024 * 1024
STAGE_SHAPE = (4, 1024, 256)
UNITS_PER_ITER = 4
RWKV_GROUP = 64


def _dot(a, b):
    return jnp.dot(a.astype(BF16), b.astype(BF16), preferred_element_type=F32)


def _dot_nt(a, b):
    return lax.dot_general(a.astype(BF16), b.astype(BF16), (((1,), (1,)), ((), ())), preferred_element_type=F32)


def _dot_tn(a, b):
    return lax.dot_general(a.astype(BF16), b.astype(BF16), (((0,), (0,)), ((), ())), preferred_element_type=F32)


def _sigmoid(x):
    return 0.5 * jnp.tanh(0.5 * x) + 0.5


def _segsum(x, seg):
    w = seg.shape[0]
    xb = x.astype(BF16)
    return jnp.concatenate([jnp.dot(xb[:, i:i + w], seg, preferred_element_type=F32)
                            for i in range(0, x.shape[1], w)], axis=1)


def _sum3(m, x):
    hi = x.astype(BF16)
    r1 = x - hi.astype(F32)
    mid = r1.astype(BF16)
    lo = (r1 - mid.astype(F32)).astype(BF16)
    return (jnp.dot(m, hi, preferred_element_type=F32) + jnp.dot(m, mid, preferred_element_type=F32)
            + jnp.dot(m, lo, preferred_element_type=F32))


def _run_staged(gens):
    gens = list(gens)
    while gens:
        for g in list(gens):
            try:
                next(g)
            except StopIteration:
                gens.remove(g)


def _row_parts(part, bb, tl, ns):
    if bb == 1:
        step = tl // ns
        return [part(slice(None), slice(i * step, (i + 1) * step)) for i in range(ns)]
    step = bb // ns
    return [part(slice(i * step, (i + 1) * step), slice(None)) for i in range(ns)]


def _cast_weights(pairs, stage, sem):
    nslot, srows, scols = stage.shape
    chunks = []
    for src, dst in pairs:
        nrow, ncol = src.shape
        for r0 in range(0, nrow, srows):
            for c0 in range(0, ncol, scols):
                chunks.append((src, dst, r0, min(srows, nrow - r0), c0, min(scols, ncol - c0)))

    def copy(i):
        src, _, r0, nr, c0, nc = chunks[i]
        return pltpu.make_async_copy(src.at[pl.ds(r0, nr), pl.ds(c0, nc)],
                                     stage.at[i % nslot, pl.ds(0, nr), pl.ds(0, nc)], sem.at[i % nslot])

    for i in range(min(nslot - 1, len(chunks))):
        copy(i).start()
    for i, (_, dst, r0, nr, c0, nc) in enumerate(chunks):
        if i + nslot - 1 < len(chunks):
            copy(i + nslot - 1).start()
        copy(i).wait()
        dst[r0:r0 + nr, c0:c0 + nc] = stage[i % nslot, 0:nr, 0:nc].astype(BF16)


def _iota(shape, axis):
    return lax.broadcasted_iota(jnp.int32, shape, axis)


def _ada_kernel(c_ref, w_ref, b_ref, o_ref, *, per_step):
    c = c_ref[...]
    d = c.shape[1]
    val = _dot(c * _sigmoid(c), w_ref[...]) + b_ref[...]
    for j in range(o_ref.shape[1] // per_step):
        @pl.when(pl.program_id(0) == j)
        def _store():
            for i in range(per_step):
                o_ref[:, j * per_step + i, :] = val[:, i * d:(i + 1) * d]


def _ada(c_all, w_ada, b_ada):
    n, d = c_all.shape
    nmod = w_ada.shape[1] // d
    per_step = 2
    return pl.pallas_call(
        functools.partial(_ada_kernel, per_step=per_step),
        out_shape=jax.ShapeDtypeStruct((n, nmod, d), F32),
        grid=(nmod // per_step,),
        in_specs=[pl.BlockSpec((n, d), lambda j: (0, 0)),
                  pl.BlockSpec((d, per_step * d), lambda j: (0, j)),
                  pl.BlockSpec((1, per_step * d), lambda j: (0, j))],
        out_specs=pl.BlockSpec((n, nmod, d), lambda j: (0, 0, 0)),
        compiler_params=pltpu.CompilerParams(dimension_semantics=("arbitrary",)),
        name="ada",
    )(c_all, w_ada, b_ada.reshape(1, nmod * d))


def _inproj_kernel(x_ref, mod_ref, g_ref, w_ref, oh_ref, or_ref, og_ref, *rest, ns):
    bb, tl, d = x_ref.shape
    if rest:
        wbf_ref, stage, sem = rest

        @pl.when((pl.program_id(0) == 0) & (pl.program_id(1) == 0))
        def _cast():
            _cast_weights([(w_ref, wbf_ref)], stage, sem)
        w_ref = wbf_ref

    def part(bsl, tsl):
        x = x_ref[bsl, tsl, :]
        nb_, nt_ = x.shape[0], x.shape[1]
        xn = x * lax.rsqrt(jnp.mean(x * x, axis=-1, keepdims=True) + RMS_EPS) * g_ref[...]
        xm = xn * (1.0 + mod_ref[bsl, 1:2, :]) + mod_ref[bsl, 0:1, :]
        xb = xm.reshape(nb_ * nt_, d).astype(BF16)
        yield
        lo = 0
        for o_ref in (oh_ref, or_ref, og_ref):
            w = o_ref.shape[-1]
            res = jnp.dot(xb, w_ref[:, lo:lo + w], preferred_element_type=F32).reshape(nb_, nt_, w)
            o_ref[bsl, tsl, :] = res.astype(o_ref.dtype)
            lo += w
            yield

    _run_staged(_row_parts(part, bb, tl, ns))


def _inproj(x, mod, mod_off, norm_g, w_in, bb, tl, ns):
    B, L, d = x.shape
    mod_blk = mod_off // bb
    const = lambda b, l: (0, 0)
    tok = lambda b, l: (b, l, 0)
    cast = w_in.dtype == F32
    out_shape = [jax.ShapeDtypeStruct((B, L, HG_W), F32),
                 jax.ShapeDtypeStruct((B, L, RB_PROJ_W), F32),
                 jax.ShapeDtypeStruct((B, L, GATE_W), GATE_DTYPE)]
    out_specs = [pl.BlockSpec((bb, tl, HG_W), tok),
                 pl.BlockSpec((bb, tl, RB_PROJ_W), tok),
                 pl.BlockSpec((bb, tl, GATE_W), tok)]
    if cast:
        w_spec = pl.BlockSpec(memory_space=pl.ANY)
        out_shape.append(jax.ShapeDtypeStruct(w_in.shape, BF16))
        out_specs.append(pl.BlockSpec(w_in.shape, const, pipeline_mode=pl.Buffered(1)))
        scratch = [pltpu.VMEM(STAGE_SHAPE, F32), pltpu.SemaphoreType.DMA((STAGE_SHAPE[0],))]
    else:
        w_spec = pl.BlockSpec((d, PROJ_W), const, pipeline_mode=pl.Buffered(1))
        scratch = []
    return pl.pallas_call(
        functools.partial(_inproj_kernel, ns=ns),
        out_shape=tuple(out_shape),
        grid=(B // bb, L // tl),
        in_specs=[pl.BlockSpec((bb, tl, d), tok),
                  pl.BlockSpec((bb, 6, d), lambda b, l: (b + mod_blk, 0, 0)),
                  pl.BlockSpec((1, d), const),
                  w_spec],
        out_specs=tuple(out_specs),
        scratch_shapes=scratch,
        compiler_params=pltpu.CompilerParams(dimension_semantics=("arbitrary", "arbitrary"),
                                             vmem_limit_bytes=VMEM_LIMIT_BYTES),
        name="inproj",
    )(x, mod, norm_g.reshape(1, d), w_in)


def _hgrn_kernel(ph_ref, s0_ref, lbp_ref, ng_ref, o_ref, s_ref, c2_s, *, layer, bb, nch, C, c):
    @pl.when(pl.program_id(1) == 0)
    def _init():
        s_ref[...] = s0_ref[...]

    hb = lbp_ref[...]
    e = jnp.exp(hb - jnp.max(hb, axis=0, keepdims=True))
    lb = jnp.sum(e[0:layer + 1], axis=0, keepdims=True) / jnp.sum(e, axis=0, keepdims=True)
    ng = ng_ref[...]
    cum_m = (_iota((C, C), 0) >= _iota((C, C), 1)).astype(BF16)
    nsub = C // c
    causal = _iota((C, C), 0) >= _iota((C, C), 1)

    def one_chunk(ci):
        r0 = pl.multiple_of(ci * C, C)
        tasks, units = [], []
        for u in range(bb):
            ph = ph_ref[u, pl.ds(r0, C), :]
            hq = ph[:, 0:HA_W]
            hf = ph[:, HA_W:2 * HA_W]
            hi = ph[:, 2 * HA_W:3 * HA_W]
            hg = ph[:, 3 * HA_W:4 * HA_W]
            q = hq * _sigmoid(hq)
            sg = 1.0 / (1.0 + jnp.exp(-hf))
            logf = jnp.log(lb + (1.0 - lb) * sg)
            b2 = _sum3(cum_m, logf) * LOG2_E
            c2 = b2 - jnp.log2((1.0 - lb) * (1.0 - sg))
            c2_s[u] = c2
            unit = dict(u=u, gate=hg * _sigmoid(hg), tasks=[])
            for h in range(HA_HEADS):
                sl = slice(h * HA_DK, (h + 1) * HA_DK)
                t = dict(u=u, h=h, q=q[:, sl], v=hi[:, sl], b2=b2[:, sl], c2=c2[:, sl])
                tasks.append(t)
                unit['tasks'].append(t)
            units.append(unit)
        for t in tasks:
            t['S'] = s_ref[t['u'], t['h']]
        for t in tasks:
            t['o'] = _dot(t['q'] * jnp.exp2(t['b2']), t['S'])
        lane_is = [_iota((c, C), 1) == j for j in range(C)]
        for t in tasks:
            q, b2, c2 = t['q'], t['b2'], t['c2']
            sl = slice(t['h'] * HA_DK, (t['h'] + 1) * HA_DK)
            rows = []
            for i in range(nsub):
                lo = i * c
                qi, bi = q[lo:lo + c], b2[lo:lo + c]
                if i > 0:
                    bref = b2[lo - 1:lo, :]
                    kt = jnp.concatenate([jnp.exp2(bref - c2[:lo]), jnp.zeros((C - lo, HA_DK), F32)], axis=0)
                    sc = _dot_nt(qi * jnp.exp2(bi - bref), kt)
                else:
                    sc = jnp.zeros((c, C), F32)
                for s in range(c):
                    col = jnp.sum(qi * jnp.exp2(bi - c2_s[t['u'], lo + s:lo + s + 1, sl]), axis=-1, keepdims=True)
                    sc = jnp.where(lane_is[lo + s], col, sc)
                rows.append(sc)
            t['scores'] = jnp.where(causal, jnp.concatenate(rows, axis=0), 0.0)
        for t in tasks:
            t['o'] = t['o'] + _dot(t['scores'], t['v'])
        for t in tasks:
            bl = t['b2'][C - 1:C, :]
            kd = jnp.exp2(bl - t['c2'])
            dcol = jnp.transpose(jnp.broadcast_to(jnp.exp2(bl), (SUBLANES, HA_DK)))[:, 0:1]
            s_ref[t['u'], t['h']] = t['S'] * dcol + _dot_tn(kd, t['v'])
        for unit in units:
            outs = [t['o'] * lax.rsqrt(jnp.mean(t['o'] * t['o'], axis=-1, keepdims=True) + RMS_EPS)
                    for t in unit['tasks']]
            o_ref[unit['u'], pl.ds(r0, C), :] = (jnp.concatenate(outs, axis=1) * ng * unit['gate']).astype(o_ref.dtype)

    def per_chunk(ci, carry):
        one_chunk(ci)
        return carry

    lax.fori_loop(0, nch, per_chunk, 0)


def _hgrn(ph, s0, lower_bounds, norm_g, layer, bb, tl, C, c, out_dtype):
    B, L, _ = ph.shape
    const = lambda b, l: (0, 0)
    st = lambda b, l: (b, 0, 0, 0)
    kern = functools.partial(_hgrn_kernel, layer=layer, bb=bb, nch=tl // C, C=C, c=c)
    return pl.pallas_call(
        kern,
        out_shape=(jax.ShapeDtypeStruct((B, L, HA_W), out_dtype),
                   jax.ShapeDtypeStruct(s0.shape, F32)),
        grid=(B // bb, L // tl),
        in_specs=[pl.BlockSpec((bb, tl, HG_W), lambda b, l: (b, l, 0)),
                  pl.BlockSpec((bb, HA_HEADS, HA_DK, HA_DK), st),
                  pl.BlockSpec(lower_bounds.shape, const),
                  pl.BlockSpec((1, HA_W), const)],
        out_specs=(pl.BlockSpec((bb, tl, HA_W), lambda b, l: (b, l, 0)),
                   pl.BlockSpec((bb, HA_HEADS, HA_DK, HA_DK), st)),
        scratch_shapes=[pltpu.VMEM((bb, C, HA_W), F32)],
        compiler_params=pltpu.CompilerParams(dimension_semantics=("parallel", "arbitrary"),
                                             vmem_limit_bytes=VMEM_LIMIT_BYTES),
        name="hgrn",
    )(ph, s0, lower_bounds, norm_g.reshape(1, HA_W))


def _rwkv_kernel(pr_ref, prev_ref, sh0_ref, s0_ref, mu_ref, w0_ref, a0_ref, wwa_ref, g2_ref, kk_ref, ka_ref,
                 rk_ref, gnw_ref, gnb_ref, o_ref, s_ref, sh_ref, s2_s, *, bb, tl, C):
    l = pl.program_id(1)
    G = RWKV_GROUP
    nb = G // C
    npair = RB_HEADS // 2
    W = RB_PROJ_W

    if nb == 1:
        @pl.when(l == 0)
        def _init():
            z = jnp.zeros((bb, RB_N, RB_N), F32)
            for p in range(npair):
                top = jnp.concatenate([s0_ref[:, 2 * p], z], axis=2)
                bot = jnp.concatenate([z, s0_ref[:, 2 * p + 1]], axis=2)
                s2_s[:, p] = jnp.concatenate([top, bot], axis=1)

    rg, cg = _iota((G, G), 0), _iota((G, G), 1)
    same_chunk = (rg // C) == (cg // C)
    cum_m = (same_chunk & (rg >= cg)).astype(BF16)
    if nb > 1:
        cum_m = jnp.concatenate([cum_m, same_chunk.astype(BF16)], axis=0)
    rp, cp = _iota((G, 2 * RB_N), 0), _iota((G, 2 * RB_N), 1) % RB_N
    same_c = (rp // C) == (cp // C)
    tri_s = same_c & ((rp % C) > (cp % C))
    tri_i = same_c & ((rp % C) >= (cp % C))
    head0 = _iota((G, 2 * RB_N), 1) < RB_N
    same_head = (_iota((2 * RB_N, 2 * RB_N), 0) // RB_N) == (_iota((2 * RB_N, 2 * RB_N), 1) // RB_N)
    seg = ((_iota((MXU_W, MXU_W), 0) // RB_N) == (_iota((MXU_W, MXU_W), 1) // RB_N)).astype(BF16)
    eye2 = jnp.where(rp == cp, 1.0, 0.0)
    is_tanh = _iota((G, RB_LORA_WA), 1) < RB_LORA_WA // 2
    nsq = int(math.log2(C)) - 1

    def stack2(x):
        return jnp.concatenate([jnp.where(head0, x, 0.0), jnp.where(head0, 0.0, x)], axis=0)

    if tl >= G:
        n_iter, n_unit = tl // G, bb
    else:
        n_unit = min(UNITS_PER_ITER, bb * tl // G)
        n_iter = bb * tl // (G * n_unit)
    sls = [slice(p * 2 * RB_N, (p + 1) * 2 * RB_N) for p in range(npair)]

    def load_rows(it, u):
        if tl >= G:
            r0 = pl.multiple_of(it * G, G)
            x = pr_ref[u, pl.ds(r0, G), :]
            before = pr_ref[u, pl.ds(jnp.maximum(r0 - 1, 0), 1), :]
            head = jnp.where(l == 0, sh0_ref[u], prev_ref[u, SUBLANES - 1:SUBLANES, :])
            first = jnp.where(it == 0, head, before)
            prev = jnp.where(_iota((G, W), 0) == 0, first, pltpu.roll(x, 1, 0))
            return x, prev, [u], (u, pl.ds(r0, G))
        b0 = pl.multiple_of((it * n_unit + u) * nb, nb)
        x = pr_ref[pl.ds(b0, nb)]
        rolled = pltpu.roll(x.reshape(G, W), 1, 0).reshape(nb, tl, W)
        prev = jnp.where(_iota((nb, tl, W), 1) == 0, sh0_ref[pl.ds(b0, nb)], rolled)
        return x.reshape(G, W), prev.reshape(G, W), [b0 + j for j in range(nb)], (pl.ds(b0, nb),)

    def step(it, carry):
        units = []
        for u in range(n_unit):
            x, xprev, bs, out_idx = load_rows(it, u)
            pm = x + (xprev - x) * mu_ref[...]
            dwa = pm[:, 3 * RB_W:3 * RB_W + RB_LORA_WA]
            units.append(dict(bs=bs, out_idx=out_idx, r=pm[:, 0:RB_W], k=pm[:, RB_W:2 * RB_W],
                              v=pm[:, 2 * RB_W:3 * RB_W], dwa=jnp.where(is_tanh, jnp.tanh(dwa), dwa),
                              sdg=_sigmoid(pm[:, 3 * RB_W + RB_LORA_WA:])))
        wa = _dot(jnp.concatenate([q['dwa'] for q in units], axis=0), wwa_ref[...])
        gate = _dot(jnp.concatenate([q['sdg'] for q in units], axis=0), g2_ref[...])
        for i, q in enumerate(units):
            rs = slice(i * G, (i + 1) * G)
            q['lw'] = -RB_DECAY_SCALE * _sigmoid(w0_ref[...] + wa[rs, 0:RB_W])
            q['a'] = _sigmoid(a0_ref[...] + wa[rs, RB_W:])
            q['gate'] = gate[rs]
            q['kk'] = q['k'] * kk_ref[...]
            q['kp'] = q['k'] * (1.0 + (q['a'] - 1.0) * ka_ref[...])
        sums = _segsum(jnp.concatenate([q['kk'] * q['kk'] for q in units]
                                       + [q['r'] * q['kp'] * rk_ref[...] for q in units], axis=0), seg)
        tasks = []
        for i, q in enumerate(units):
            kk = q['kk'] * lax.rsqrt(jnp.maximum(sums[i * G:(i + 1) * G], RB_KK_EPS * RB_KK_EPS))
            q['bonus'] = sums[(n_unit + i) * G:(n_unit + i + 1) * G] * q['v']
            kb = kk * q['a']
            lw, kp = q['lw'], q['kp']
            gs = _sum3(cum_m, lw)
            g = gs[:G]
            gl = gs[G:] if nb > 1 else g[G - 1:G, :]
            eg, eng, egl = jnp.exp(g), jnp.exp(-g), jnp.exp(gl)
            At = -kk * (eg * jnp.exp(-lw))
            Bt = kb * eng
            Kt = kp * eng
            Rt = q['r'] * eg
            dec = egl * eng
            Ke = kp * dec
            Be = kb * dec
            q['tasks'] = []
            for p in range(npair):
                sl = sls[p]
                t = dict(p=p, bs=q['bs'], AR=jnp.concatenate([At[:, sl], Rt[:, sl]], axis=0), B=Bt[:, sl],
                         K=Kt[:, sl], V=q['v'][:, sl], Ke=Ke[:, sl], Be=Be[:, sl], egl=egl[:, sl])
                tasks.append(t)
                q['tasks'].append(t)
        for t in tasks:
            if nb == 1:
                t['S2'] = [s2_s[b, t['p']] for b in t['bs']]
            else:
                t['S'] = [[s0_ref[b, 2 * t['p'] + h] for h in range(2)] for b in t['bs']]
        for t in tasks:
            t['M'] = _dot_nt(t['AR'], jnp.concatenate([stack2(t['B']), stack2(t['K'])], axis=0))
        for t in tasks:
            if nb == 1:
                x0 = _dot_nt(t['AR'], t['S2'][0])
                t['x0a'], t['x0r'] = x0[:G], x0[G:]
            else:
                x0 = []
                for j in range(nb):
                    arj = jnp.concatenate([t['AR'][j * C:(j + 1) * C], t['AR'][G + j * C:G + (j + 1) * C]], axis=0)
                    x0.append(jnp.concatenate([_dot_nt(arj[:, h * RB_N:(h + 1) * RB_N], t['S'][j][h])
                                               for h in range(2)], axis=1))
                t['x0a'] = jnp.concatenate([x[:C] for x in x0], axis=0)
                t['x0r'] = jnp.concatenate([x[C:] for x in x0], axis=0)
        for t in tasks:
            M = t['M']
            lak = jnp.where(tri_s, M[:G, 2 * RB_N:], 0.0)
            lrk = jnp.where(tri_i, M[G:, 2 * RB_N:], 0.0)
            lv = _dot(jnp.concatenate([lak, lrk], axis=0), stack2(t['V']))
            t['X'] = t['x0a'] + lv[:G]
            t['Yv'] = t['x0r'] + lv[G:]
            t['P'] = jnp.where(tri_s, M[:G, :2 * RB_N], 0.0)
        for t in tasks:
            t['T'] = eye2 + t['P']
        for k in range(nsq):
            for t in tasks:
                rhs = stack2(t['P'])
                if k == 0:
                    t['P'] = _dot(t['P'], rhs)
                else:
                    pt = _dot(jnp.concatenate([t['P'], t['T']], axis=0), rhs)
                    t['P'] = pt[:G]
                    t['T'] = t['T'] + pt[G:]
        for t in tasks:
            t['T'] = t['T'] + _dot(t['T'], stack2(t['P']))
        for t in tasks:
            t['SA'] = _dot(t['T'], stack2(t['X']))
        for t in tasks:
            lrb = jnp.where(tri_i, t['M'][G:, :2 * RB_N], 0.0)
            t['y'] = t['Yv'] + _dot(lrb, stack2(t['SA']))
        for t in tasks:
            for j in range(nb):
                js = slice(j * C, (j + 1) * C)
                lhs = jnp.concatenate([t['V'][js], t['SA'][js]], axis=0)
                rhs = jnp.concatenate([t['Ke'][js], t['Be'][js]], axis=0)
                egl_j = t['egl'][j * C:j * C + 1, :]
                if nb == 1:
                    s2_s[t['bs'][j], t['p']] = t['S2'][j] * egl_j + jnp.where(same_head, _dot_tn(lhs, rhs), 0.0)
                else:
                    for h in range(2):
                        hs = slice(h * RB_N, (h + 1) * RB_N)
                        s_ref[t['bs'][j], 2 * t['p'] + h] = t['S'][j][h] * egl_j[:, hs] + _dot_tn(lhs[:, hs], rhs[:, hs])
        y = jnp.concatenate([jnp.concatenate([t['y'] for t in q['tasks']], axis=1) for q in units], axis=0)
        yc = y - _segsum(y, seg) * (1.0 / RB_N)
        var = _segsum(yc * yc, seg) * (1.0 / RB_N)
        yn = yc * lax.rsqrt(var + RB_GN_EPS) * gnw_ref[...] + gnb_ref[...]
        for i, q in enumerate(units):
            out = ((yn[i * G:(i + 1) * G] + q['bonus']) * q['gate']).astype(o_ref.dtype)
            if tl >= G:
                o_ref[q['out_idx']] = out
            else:
                o_ref[q['out_idx']] = out.reshape(nb, tl, RB_W)
        return carry

    lax.fori_loop(0, n_iter, step, 0)

    @pl.when(l == pl.num_programs(1) - 1)
    def _fin():
        sh_ref[...] = pr_ref[:, tl - 1:tl, :]
        if nb == 1:
            for p in range(npair):
                S2 = s2_s[:, p]
                s_ref[:, 2 * p] = S2[:, :RB_N, :RB_N]
                s_ref[:, 2 * p + 1] = S2[:, RB_N:, RB_N:]


def _rwkv(pr, shift0, s0, mu, w0, a0, wwa, g2, k_k, k_a, r_k, gn_w, gn_b, bb, tl, C, out_dtype):
    B, L, _ = pr.shape
    const = lambda b, l: (0, 0)
    tok = lambda b, l: (b, l, 0)
    st = lambda b, l: (b, 0, 0, 0)
    vec = lambda x: x.reshape(1, -1)
    kern = functools.partial(_rwkv_kernel, bb=bb, tl=tl, C=C)
    assert C == RWKV_GROUP or (tl == C and L == tl)
    pair_rows = bb if C == RWKV_GROUP else 1
    return pl.pallas_call(
        kern,
        out_shape=(jax.ShapeDtypeStruct((B, L, RB_W), out_dtype),
                   jax.ShapeDtypeStruct(s0.shape, F32),
                   jax.ShapeDtypeStruct((B, 1, RB_PROJ_W), F32)),
        grid=(B // bb, L // tl),
        in_specs=[pl.BlockSpec((bb, tl, RB_PROJ_W), tok),
                  pl.BlockSpec((bb, SUBLANES, RB_PROJ_W), lambda b, l: (b, jnp.maximum(l * (tl // SUBLANES) - 1, 0), 0)),
                  pl.BlockSpec((bb, 1, RB_PROJ_W), lambda b, l: (b, 0, 0)),
                  pl.BlockSpec((bb, RB_HEADS, RB_N, RB_N), st),
                  pl.BlockSpec((1, RB_PROJ_W), const),
                  pl.BlockSpec((1, RB_W), const),
                  pl.BlockSpec((1, RB_W), const),
                  pl.BlockSpec((RB_LORA_WA, 2 * RB_W), const),
                  pl.BlockSpec((RB_LORA_G, RB_W), const),
                  pl.BlockSpec((1, RB_W), const),
                  pl.BlockSpec((1, RB_W), const),
                  pl.BlockSpec((1, RB_W), const),
                  pl.BlockSpec((1, RB_W), const),
                  pl.BlockSpec((1, RB_W), const)],
        out_specs=(pl.BlockSpec((bb, tl, RB_W), tok),
                   pl.BlockSpec((bb, RB_HEADS, RB_N, RB_N), st),
                   pl.BlockSpec((bb, 1, RB_PROJ_W), lambda b, l: (b, 0, 0))),
        scratch_shapes=[pltpu.VMEM((pair_rows, RB_HEADS // 2, 2 * RB_N, 2 * RB_N), F32)],
        compiler_params=pltpu.CompilerParams(dimension_semantics=("parallel", "arbitrary"),
                                             vmem_limit_bytes=VMEM_LIMIT_BYTES),
        name="rwkv",
    )(pr, pr, shift0.reshape(B, 1, RB_PROJ_W), s0, vec(mu), vec(w0), vec(a0), wwa, g2, vec(k_k), vec(k_a),
      vec(r_k), vec(gn_w), vec(gn_b))


def _mixffn_kernel(oa_ref, ob_ref, pg_ref, x_ref, mod_ref, nf_ref, nfin_ref, wua_ref, wub_ref, wo_ref, wfi_ref,
                   wfo_ref, y_ref, *rest, ns):
    bb, tl, d = x_ref.shape
    if rest:
        wbf, (stage, sem) = rest[:5], rest[5:]

        @pl.when((pl.program_id(0) == 0) & (pl.program_id(1) == 0))
        def _cast():
            _cast_weights(list(zip((wua_ref, wub_ref, wo_ref, wfi_ref, wfo_ref), wbf)), stage, sem)
        wua_ref, wub_ref, wo_ref, wfi_ref, wfo_ref = wbf

    def part(bsl, tsl):
        rows = lambda ref: ref[bsl, tsl, :]
        x = rows(x_ref)
        nb_, nt_ = x.shape[0], x.shape[1]
        R = nb_ * nt_
        mod = lambda i: mod_ref[bsl, i:i + 1, :]
        pg = rows(pg_ref).astype(F32).reshape(R, GATE_W)
        ua = _dot(rows(oa_ref).reshape(R, HA_W), wua_ref[...])
        ub = _dot(rows(ob_ref).reshape(R, RB_W), wub_ref[...])
        yield
        merged = _sigmoid(pg[:, :d]) * ua + _sigmoid(pg[:, d:]) * ub
        mix = _dot(merged, wo_ref[...]).reshape(nb_, nt_, d)
        yield
        x1 = x + mod(2) * mix
        xn = x1 * lax.rsqrt(jnp.mean(x1 * x1, axis=-1, keepdims=True) + RMS_EPS) * nf_ref[...]
        xf = (xn * (1.0 + mod(4)) + mod(3)).reshape(R, d)
        f = None
        for lo, hi in FFN_GROUPS:
            gate = _dot(xf, wfi_ref[:, lo:hi])
            up = _dot(xf, wfi_ref[:, FFN_HIDDEN + lo:FFN_HIDDEN + hi])
            part = _dot(gate * _sigmoid(gate) * up, wfo_ref[lo:hi, :])
            f = part if f is None else f + part
            yield
        f = f.reshape(nb_, nt_, d)
        x2 = x1 + mod(5) * f
        y_ref[bsl, tsl, :] = x2 * lax.rsqrt(jnp.mean(x2 * x2, axis=-1, keepdims=True) + RMS_EPS) * nfin_ref[...]

    _run_staged(_row_parts(part, bb, tl, ns))


def _mixffn(oa, ob, pg, x, mod, mod_off, norm_ffn, norm_final, weights, bb, tl, ns):
    B, L, d = x.shape
    mod_blk = mod_off // bb
    const = lambda b, l: (0, 0)
    tok = lambda b, l: (b, l, 0)
    cast = weights[0].dtype == F32
    out_shape = [jax.ShapeDtypeStruct((B, L, d), F32)]
    out_specs = [pl.BlockSpec((bb, tl, d), tok)]
    if cast:
        w_specs = [pl.BlockSpec(memory_space=pl.ANY) for _ in weights]
        out_shape += [jax.ShapeDtypeStruct(w.shape, BF16) for w in weights]
        out_specs += [pl.BlockSpec(w.shape, const, pipeline_mode=pl.Buffered(1)) for w in weights]
        scratch = [pltpu.VMEM(STAGE_SHAPE, F32), pltpu.SemaphoreType.DMA((STAGE_SHAPE[0],))]
    else:
        w_specs = [pl.BlockSpec(w.shape, const, pipeline_mode=pl.Buffered(1)) for w in weights]
        scratch = []
    return pl.pallas_call(
        functools.partial(_mixffn_kernel, ns=ns),
        out_shape=tuple(out_shape),
        grid=(B // bb, L // tl),
        in_specs=[pl.BlockSpec((bb, tl, HA_W), tok),
                  pl.BlockSpec((bb, tl, RB_W), tok),
                  pl.BlockSpec((bb, tl, GATE_W), tok),
                  pl.BlockSpec((bb, tl, d), tok),
                  pl.BlockSpec((bb, 6, d), lambda b, l: (b + mod_blk, 0, 0)),
                  pl.BlockSpec((1, d), const),
                  pl.BlockSpec((1, d), const)] + w_specs,
        out_specs=tuple(out_specs),
        scratch_shapes=scratch,
        compiler_params=pltpu.CompilerParams(dimension_semantics=("arbitrary", "arbitrary"),
                                             vmem_limit_bytes=VMEM_LIMIT_BYTES),
        name="mixffn",
    )(oa, ob, pg, x, mod, norm_ffn.reshape(1, d), norm_final.reshape(1, d), *weights)


def _block(x, mod, mod_off, s_h, s_r, shift, p, w_in, w_mix, layer, tiles):
    assert mod_off % tiles['inproj'][0] == 0 and mod_off % tiles['mixffn'][0] == 0
    ph, pr, pg, *w_in_bf = _inproj(x, mod, mod_off, p['norm_mix'], w_in, *tiles['inproj'])
    oa, s_h = _hgrn(ph, s_h, p['lower_bounds'], p['hgrn_norm'], layer, *tiles['hgrn'])
    ob, s_r, shift = _rwkv(pr, shift, s_r, p['mu'], p['w0'], p['a0'], p['wwa'], p['g2'], p['k_k'], p['k_a'], p['r_k'],
                           p['gn_w'], p['gn_b'], *tiles['rwkv'])
    y, *w_mix_bf = _mixffn(oa, ob, pg, x, mod, mod_off, p['norm_ffn'], p['norm_final'], w_mix, *tiles['mixffn'])
    return (y, s_h, s_r, shift.reshape(shift.shape[0], RB_PROJ_W)), (w_in_bf[0] if w_in_bf else w_in), (w_mix_bf or w_mix)


PROMPT_TILES = {'inproj': (1, 512, 4), 'mixffn': (1, 512, 2), 'hgrn': (8, 128, 64, 8, BF16), 'rwkv': (4, 256, 64, BF16)}
SAMPLE_TILES = {'inproj': (64, 8, 2), 'mixffn': (64, 8, 2), 'hgrn': (16, 8, 8, 8, F32), 'rwkv': (32, 8, 8, F32)}


def kernel(x_prompt, x_sample, c_prompt, c_sample, state_hgrn, state_rwkv, state_shift, w_ada, b_ada, norm_mix, norm_ffn, w_in, hgrn_lower_bounds, hgrn_norm, rwkv_mu, rwkv_w0, rwkv_w2, rwkv_a0, rwkv_a2, rwkv_g2, rwkv_k_k, rwkv_k_a, rwkv_r_k, rwkv_gn_w, rwkv_gn_b, w_up_a, w_up_b, w_out, w_ffn_in, w_ffn_out, norm_final):
    depth = w_ada.shape[0]
    assert depth == 1, "the final norm is fused into the layer kernel"
    nbp, nbs = x_prompt.shape[0], x_sample.shape[0]

    hp = jnp.zeros((nbp, HA_HEADS, HA_DK, HA_DK), F32)
    rp = jnp.zeros((nbp, RB_HEADS, RB_N, RB_N), F32)
    sp = jnp.zeros((nbp, RB_PROJ_W), x_prompt.dtype)

    yp, ys = x_prompt, x_sample
    outs = [[] for _ in range(6)]
    for l in range(depth):
        half = RB_LORA_WA // 2
        zero = jnp.zeros((half, RB_W), F32)
        wwa = jnp.concatenate([jnp.concatenate([rwkv_w2[l], zero], axis=1),
                               jnp.concatenate([zero, rwkv_a2[l]], axis=1)], axis=0)
        p = {
            'norm_mix': norm_mix[l], 'norm_ffn': norm_ffn[l], 'norm_final': norm_final,
            'lower_bounds': hgrn_lower_bounds, 'hgrn_norm': hgrn_norm[l],
            'mu': rwkv_mu[l], 'w0': rwkv_w0[l], 'a0': rwkv_a0[l], 'wwa': wwa.astype(BF16),
            'g2': rwkv_g2[l].astype(BF16), 'k_k': rwkv_k_k[l], 'k_a': rwkv_k_a[l], 'r_k': rwkv_r_k[l],
            'gn_w': rwkv_gn_w[l], 'gn_b': rwkv_gn_b[l],
        }
        w_mix = (w_up_a[l], w_up_b[l], w_out[l], w_ffn_in[l], w_ffn_out[l])
        mod = _ada(jnp.concatenate([c_sample, c_prompt], axis=0), w_ada[l], b_ada[l])
        (yp, h1, r1, s1), w_in_bf, w_mix_bf = _block(yp, mod, nbs, hp, rp, sp, p, w_in[l], w_mix, l, PROMPT_TILES)
        (ys, h2, r2, s2), _, _ = _block(ys, mod, 0, state_hgrn[l], state_rwkv[l], state_shift[l], p, w_in_bf,
                                        tuple(w_mix_bf), l, SAMPLE_TILES)
        for lst, val in zip(outs, (h1, r1, s1, h2, r2, s2)):
            lst.append(val)
    stacked = [jnp.stack(o) for o in outs]
    return (yp, ys, *stacked)
```

```python
import functools
import math

import jax
import jax.numpy as jnp
from jax import lax
from jax.experimental import pallas as pl
from jax.experimental.pallas import tpu as pltpu

F32 = jnp.float32
BF16 = jnp.bfloat16

D_MODEL = 1024
HA_HEADS = 4
HA_DK = 128
HA_W = HA_HEADS * HA_DK
RB_HEADS = 8
RB_N = 64
RB_W = RB_HEADS * RB_N
RB_LORA_WA = 128
RB_LORA_G = 128
RB_PROJ_W = 3 * RB_W + RB_LORA_WA + RB_LORA_G
RB_DECAY_SCALE = math.exp(-0.5)
RB_GN_EPS = 64e-5
HG_W = 4 * HA_W
GATE_W = 2 * D_MODEL
PROJ_W = HG_W + RB_PROJ_W + GATE_W
FFN_HIDDEN = 2816
RMS_EPS = 1e-6
LOG2_E = math.log2(math.e)
GATE_DTYPE = BF16

MXU_W = 256
FFN_GROUPS = ((0, 6 * MXU_W), (6 * MXU_W, FFN_HIDDEN))
SUBLANES = 8
RB_KK_EPS = 1e-12
VMEM_LIMIT_BYTES = 58 * 1024 * 1024
STAGE_SHAPE = (4, 1024, 256)
UNITS_PER_ITER = 4
RWKV_GROUP = 64


def _dot(a, b):
    return jnp.dot(a.astype(BF16), b.astype(BF16), preferred_element_type=F32)


def _dot_nt(a, b):
    return lax.dot_general(a.astype(BF16), b.astype(BF16), (((1,), (1,)), ((), ())), preferred_element_type=F32)


def _dot_tn(a, b):
    return lax.dot_general(a.astype(BF16), b.astype(BF16), (((0,), (0,)), ((), ())), preferred_element_type=F32)


def _sigmoid(x):
    return 0.5 * jnp.tanh(0.5 * x) + 0.5


def _segsum(x, seg):
    w = seg.shape[0]
    xb = x.astype(BF16)
    return jnp.concatenate([jnp.dot(xb[:, i:i + w], seg, preferred_element_type=F32)
                            for i in range(0, x.shape[1], w)], axis=1)


def _sum3(m, x):
    hi = x.astype(BF16)
    r1 = x - hi.astype(F32)
    mid = r1.astype(BF16)
    lo = (r1 - mid.astype(F32)).astype(BF16)
    return (jnp.dot(m, hi, preferred_element_type=F32) + jnp.dot(m, mid, preferred_element_type=F32)
            + jnp.dot(m, lo, preferred_element_type=F32))


def _run_staged(gens):
    gens = list(gens)
    while gens:
        for g in list(gens):
            try:
                next(g)
            except StopIteration:
                gens.remove(g)


def _row_parts(part, bb, tl, ns):
    if bb == 1:
        step = tl // ns
        return [part(slice(None), slice(i * step, (i + 1) * step)) for i in range(ns)]
    step = bb // ns
    return [part(slice(i * step, (i + 1) * step), slice(None)) for i in range(ns)]


def _cast_weights(pairs, stage, sem):
    nslot, srows, scols = stage.shape
    chunks = []
    for src, dst in pairs:
        nrow, ncol = src.shape
        for r0 in range(0, nrow, srows):
            for c0 in range(0, ncol, scols):
                chunks.append((src, dst, r0, min(srows, nrow - r0), c0, min(scols, ncol - c0)))

    def copy(i):
        src, _, r0, nr, c0, nc = chunks[i]
        return pltpu.make_async_copy(src.at[pl.ds(r0, nr), pl.ds(c0, nc)],
                                     stage.at[i % nslot, pl.ds(0, nr), pl.ds(0, nc)], sem.at[i % nslot])

    for i in range(min(nslot, len(chunks))):
        copy(i).start()
    for i, (_, dst, r0, nr, c0, nc) in enumerate(chunks):
        copy(i).wait()
        dst[r0:r0 + nr, c0:c0 + nc] = stage[i % nslot, 0:nr, 0:nc].astype(BF16)
        if i + nslot < len(chunks):
            copy(i + nslot).start()


def _iota(shape, axis):
    return lax.broadcasted_iota(jnp.int32, shape, axis)


def _ada_kernel(c_ref, w_ref, b_ref, o_ref, *, per_step):
    c = c_ref[...]
    d = c.shape[1]
    val = _dot(c * _sigmoid(c), w_ref[...]) + b_ref[...]
    for j in range(o_ref.shape[1] // per_step):
        @pl.when(pl.program_id(0) == j)
        def _store():
            for i in range(per_step):
                o_ref[:, j * per_step + i, :] = val[:, i * d:(i + 1) * d]


def _ada(c_all, w_ada, b_ada):
    n, d = c_all.shape
    nmod = w_ada.shape[1] // d
    per_step = 2
    return pl.pallas_call(
        functools.partial(_ada_kernel, per_step=per_step),
        out_shape=jax.ShapeDtypeStruct((n, nmod, d), F32),
        grid=(nmod // per_step,),
        in_specs=[pl.BlockSpec((n, d), lambda j: (0, 0)),
                  pl.BlockSpec((d, per_step * d), lambda j: (0, j)),
                  pl.BlockSpec((1, per_step * d), lambda j: (0, j))],
        out_specs=pl.BlockSpec((n, nmod, d), lambda j: (0, 0, 0)),
        compiler_params=pltpu.CompilerParams(dimension_semantics=("arbitrary",)),
        name="ada",
    )(c_all, w_ada, b_ada.reshape(1, nmod * d))


def _inproj_kernel(x_ref, mod_ref, g_ref, w_ref, oh_ref, or_ref, og_ref, *rest, ns):
    bb, tl, d = x_ref.shape
    if rest:
        wbf_ref, stage, sem = rest

        @pl.when((pl.program_id(0) == 0) & (pl.program_id(1) == 0))
        def _cast():
            _cast_weights([(w_ref, wbf_ref)], stage, sem)
        w_ref = wbf_ref

    def part(bsl, tsl):
        x = x_ref[bsl, tsl, :]
        nb_, nt_ = x.shape[0], x.shape[1]
        xn = x * lax.rsqrt(jnp.mean(x * x, axis=-1, keepdims=True) + RMS_EPS) * g_ref[...]
        xm = xn * (1.0 + mod_ref[bsl, 1:2, :]) + mod_ref[bsl, 0:1, :]
        xb = xm.reshape(nb_ * nt_, d).astype(BF16)
        yield
        lo = 0
        for o_ref in (oh_ref, or_ref, og_ref):
            w = o_ref.shape[-1]
            res = jnp.dot(xb, w_ref[:, lo:lo + w], preferred_element_type=F32).reshape(nb_, nt_, w)
            o_ref[bsl, tsl, :] = res.astype(o_ref.dtype)
            lo += w
            yield

    _run_staged(_row_parts(part, bb, tl, ns))


def _inproj(x, mod, mod_off, norm_g, w_in, bb, tl, ns):
    B, L, d = x.shape
    mod_blk = mod_off // bb
    const = lambda b, l: (0, 0)
    tok = lambda b, l: (b, l, 0)
    cast = w_in.dtype == F32
    out_shape = [jax.ShapeDtypeStruct((B, L, HG_W), F32),
                 jax.ShapeDtypeStruct((B, L, RB_PROJ_W), F32),
                 jax.ShapeDtypeStruct((B, L, GATE_W), GATE_DTYPE)]
    out_specs = [pl.BlockSpec((bb, tl, HG_W), tok),
                 pl.BlockSpec((bb, tl, RB_PROJ_W), tok),
                 pl.BlockSpec((bb, tl, GATE_W), tok)]
    if cast:
        w_spec = pl.BlockSpec(memory_space=pl.ANY)
        out_shape.append(jax.ShapeDtypeStruct(w_in.shape, BF16))
        out_specs.append(pl.BlockSpec(w_in.shape, const, pipeline_mode=pl.Buffered(1)))
        scratch = [pltpu.VMEM(STAGE_SHAPE, F32), pltpu.SemaphoreType.DMA((STAGE_SHAPE[0],))]
    else:
        w_spec = pl.BlockSpec((d, PROJ_W), const, pipeline_mode=pl.Buffered(1))
        scratch = []
    return pl.pallas_call(
        functools.partial(_inproj_kernel, ns=ns),
        out_shape=tuple(out_shape),
        grid=(B // bb, L // tl),
        in_specs=[pl.BlockSpec((bb, tl, d), tok),
                  pl.BlockSpec((bb, 6, d), lambda b, l: (b + mod_blk, 0, 0)),
                  pl.BlockSpec((1, d), const),
                  w_spec],
        out_specs=tuple(out_specs),
        scratch_shapes=scratch,
        compiler_params=pltpu.CompilerParams(dimension_semantics=("arbitrary", "arbitrary"),
                                             vmem_limit_bytes=VMEM_LIMIT_BYTES),
        name="inproj",
    )(x, mod, norm_g.reshape(1, d), w_in)


def _hgrn_kernel(ph_ref, s0_ref, lbp_ref, ng_ref, o_ref, s_ref, c2_s, *, layer, bb, nch, C, c):
    @pl.when(pl.program_id(1) == 0)
    def _init():
        s_ref[...] = s0_ref[...]

    hb = lbp_ref[...]
    e = jnp.exp(hb - jnp.max(hb, axis=0, keepdims=True))
    lb = jnp.sum(e[0:layer + 1], axis=0, keepdims=True) / jnp.sum(e, axis=0, keepdims=True)
    ng = ng_ref[...]
    cum_m = (_iota((C, C), 0) >= _iota((C, C), 1)).astype(BF16)
    nsub = C // c
    causal = _iota((C, C), 0) >= _iota((C, C), 1)

    def one_chunk(ci):
        r0 = pl.multiple_of(ci * C, C)
        tasks, units = [], []
        for u in range(bb):
            ph = ph_ref[u, pl.ds(r0, C), :]
            hq = ph[:, 0:HA_W]
            hf = ph[:, HA_W:2 * HA_W]
            hi = ph[:, 2 * HA_W:3 * HA_W]
            hg = ph[:, 3 * HA_W:4 * HA_W]
            q = hq * _sigmoid(hq)
            sg = 1.0 / (1.0 + jnp.exp(-hf))
            logf = jnp.log(lb + (1.0 - lb) * sg)
            b2 = _sum3(cum_m, logf) * LOG2_E
            c2 = b2 - jnp.log2((1.0 - lb) * (1.0 - sg))
            c2_s[u] = c2
            unit = dict(u=u, gate=hg * _sigmoid(hg), tasks=[])
            for h in range(HA_HEADS):
                sl = slice(h * HA_DK, (h + 1) * HA_DK)
                t = dict(u=u, h=h, q=q[:, sl], v=hi[:, sl], b2=b2[:, sl], c2=c2[:, sl])
                tasks.append(t)
                unit['tasks'].append(t)
            units.append(unit)
        for t in tasks:
            t['S'] = s_ref[t['u'], t['h']]
        for t in tasks:
            t['o'] = _dot(t['q'] * jnp.exp2(t['b2']), t['S'])
        lane_is = [_iota((c, C), 1) == j for j in range(C)]
        for t in tasks:
            q, b2, c2 = t['q'], t['b2'], t['c2']
            sl = slice(t['h'] * HA_DK, (t['h'] + 1) * HA_DK)
            rows = []
            for i in range(nsub):
                lo = i * c
                qi, bi = q[lo:lo + c], b2[lo:lo + c]
                if i > 0:
                    bref = b2[lo - 1:lo, :]
                    kt = jnp.concatenate([jnp.exp2(bref - c2[:lo]), jnp.zeros((C - lo, HA_DK), F32)], axis=0)
                    sc = _dot_nt(qi * jnp.exp2(bi - bref), kt)
                else:
                    sc = jnp.zeros((c, C), F32)
                for s in range(c):
                    col = jnp.sum(qi * jnp.exp2(bi - c2_s[t['u'], lo + s:lo + s + 1, sl]), axis=-1, keepdims=True)
                    sc = jnp.where(lane_is[lo + s], col, sc)
                rows.append(sc)
            t['scores'] = jnp.where(causal, jnp.concatenate(rows, axis=0), 0.0)
        for t in tasks:
            t['o'] = t['o'] + _dot(t['scores'], t['v'])
        for t in tasks:
            bl = t['b2'][C - 1:C, :]
            kd = jnp.exp2(bl - t['c2'])
            dcol = jnp.transpose(jnp.broadcast_to(jnp.exp2(bl), (SUBLANES, HA_DK)))[:, 0:1]
            s_ref[t['u'], t['h']] = t['S'] * dcol + _dot_tn(kd, t['v'])
        for unit in units:
            outs = [t['o'] * lax.rsqrt(jnp.mean(t['o'] * t['o'], axis=-1, keepdims=True) + RMS_EPS)
                    for t in unit['tasks']]
            o_ref[unit['u'], pl.ds(r0, C), :] = (jnp.concatenate(outs, axis=1) * ng * unit['gate']).astype(o_ref.dtype)

    def per_chunk(ci, carry):
        one_chunk(ci)
        return carry

    lax.fori_loop(0, nch, per_chunk, 0)


def _hgrn(ph, s0, lower_bounds, norm_g, layer, bb, tl, C, c, out_dtype):
    B, L, _ = ph.shape
    const = lambda b, l: (0, 0)
    st = lambda b, l: (b, 0, 0, 0)
    kern = functools.partial(_hgrn_kernel, layer=layer, bb=bb, nch=tl // C, C=C, c=c)
    return pl.pallas_call(
        kern,
        out_shape=(jax.ShapeDtypeStruct((B, L, HA_W), out_dtype),
                   jax.ShapeDtypeStruct(s0.shape, F32)),
        grid=(B // bb, L // tl),
        in_specs=[pl.BlockSpec((bb, tl, HG_W), lambda b, l: (b, l, 0)),
                  pl.BlockSpec((bb, HA_HEADS, HA_DK, HA_DK), st),
                  pl.BlockSpec(lower_bounds.shape, const),
                  pl.BlockSpec((1, HA_W), const)],
        out_specs=(pl.BlockSpec((bb, tl, HA_W), lambda b, l: (b, l, 0)),
                   pl.BlockSpec((bb, HA_HEADS, HA_DK, HA_DK), st)),
        scratch_shapes=[pltpu.VMEM((bb, C, HA_W), F32)],
        compiler_params=pltpu.CompilerParams(dimension_semantics=("parallel", "arbitrary"),
                                             vmem_limit_bytes=VMEM_LIMIT_BYTES),
        name="hgrn",
    )(ph, s0, lower_bounds, norm_g.reshape(1, HA_W))


def _rwkv_kernel(pr_ref, prev_ref, sh0_ref, s0_ref, mu_ref, w0_ref, a0_ref, wwa_ref, g2_ref, kk_ref, ka_ref,
                 rk_ref, gnw_ref, gnb_ref, o_ref, s_ref, sh_ref, s2_s, *, bb, tl, C):
    l = pl.program_id(1)
    G = RWKV_GROUP
    nb = G // C
    npair = RB_HEADS // 2
    W = RB_PROJ_W

    if nb == 1:
        @pl.when(l == 0)
        def _init():
            z = jnp.zeros((bb, RB_N, RB_N), F32)
            for p in range(npair):
                top = jnp.concatenate([s0_ref[:, 2 * p], z], axis=2)
                bot = jnp.concatenate([z, s0_ref[:, 2 * p + 1]], axis=2)
                s2_s[:, p] = jnp.concatenate([top, bot], axis=1)

    rg, cg = _iota((G, G), 0), _iota((G, G), 1)
    same_chunk = (rg // C) == (cg // C)
    cum_m = (same_chunk & (rg >= cg)).astype(BF16)
    if nb > 1:
        cum_m = jnp.concatenate([cum_m, same_chunk.astype(BF16)], axis=0)
    rp, cp = _iota((G, 2 * RB_N), 0), _iota((G, 2 * RB_N), 1) % RB_N
    same_c = (rp // C) == (cp // C)
    tri_s = same_c & ((rp % C) > (cp % C))
    tri_i = same_c & ((rp % C) >= (cp % C))
    head0 = _iota((G, 2 * RB_N), 1) < RB_N
    same_head = (_iota((2 * RB_N, 2 * RB_N), 0) // RB_N) == (_iota((2 * RB_N, 2 * RB_N), 1) // RB_N)
    seg = ((_iota((MXU_W, MXU_W), 0) // RB_N) == (_iota((MXU_W, MXU_W), 1) // RB_N)).astype(BF16)
    eye2 = jnp.where(rp == cp, 1.0, 0.0)
    is_tanh = _iota((G, RB_LORA_WA), 1) < RB_LORA_WA // 2
    nsq = int(math.log2(C)) - 1

    def stack2(x):
        return jnp.concatenate([jnp.where(head0, x, 0.0), jnp.where(head0, 0.0, x)], axis=0)

    if tl >= G:
        n_iter, n_unit = tl // G, bb
    else:
        n_unit = min(UNITS_PER_ITER, bb * tl // G)
        n_iter = bb * tl // (G * n_unit)
    sls = [slice(p * 2 * RB_N, (p + 1) * 2 * RB_N) for p in range(npair)]

    def load_rows(it, u):
        if tl >= G:
            r0 = pl.multiple_of(it * G, G)
            x = pr_ref[u, pl.ds(r0, G), :]
            before = pr_ref[u, pl.ds(jnp.maximum(r0 - 1, 0), 1), :]
            head = jnp.where(l == 0, sh0_ref[u], prev_ref[u, SUBLANES - 1:SUBLANES, :])
            first = jnp.where(it == 0, head, before)
            prev = jnp.where(_iota((G, W), 0) == 0, first, pltpu.roll(x, 1, 0))
            return x, prev, [u], (u, pl.ds(r0, G))
        b0 = pl.multiple_of((it * n_unit + u) * nb, nb)
        x = pr_ref[pl.ds(b0, nb)]
        rolled = pltpu.roll(x.reshape(G, W), 1, 0).reshape(nb, tl, W)
        prev = jnp.where(_iota((nb, tl, W), 1) == 0, sh0_ref[pl.ds(b0, nb)], rolled)
        return x.reshape(G, W), prev.reshape(G, W), [b0 + j for j in range(nb)], (pl.ds(b0, nb),)

    def step(it, carry):
        units = []
        for u in range(n_unit):
            x, xprev, bs, out_idx = load_rows(it, u)
            pm = x + (xprev - x) * mu_ref[...]
            dwa = pm[:, 3 * RB_W:3 * RB_W + RB_LORA_WA]
            units.append(dict(bs=bs, out_idx=out_idx, r=pm[:, 0:RB_W], k=pm[:, RB_W:2 * RB_W],
                              v=pm[:, 2 * RB_W:3 * RB_W], dwa=jnp.where(is_tanh, jnp.tanh(dwa), dwa),
                              sdg=_sigmoid(pm[:, 3 * RB_W + RB_LORA_WA:])))
        wa = _dot(jnp.concatenate([q['dwa'] for q in units], axis=0), wwa_ref[...])
        gate = _dot(jnp.concatenate([q['sdg'] for q in units], axis=0), g2_ref[...])
        for i, q in enumerate(units):
            rs = slice(i * G, (i + 1) * G)
            q['lw'] = -RB_DECAY_SCALE * _sigmoid(w0_ref[...] + wa[rs, 0:RB_W])
            q['a'] = _sigmoid(a0_ref[...] + wa[rs, RB_W:])
            q['gate'] = gate[rs]
            q['kk'] = q['k'] * kk_ref[...]
            q['kp'] = q['k'] * (1.0 + (q['a'] - 1.0) * ka_ref[...])
        sums = _segsum(jnp.concatenate([q['kk'] * q['kk'] for q in units]
                                       + [q['r'] * q['kp'] * rk_ref[...] for q in units], axis=0), seg)
        tasks = []
        for i, q in enumerate(units):
            kk = q['kk'] * lax.rsqrt(jnp.maximum(sums[i * G:(i + 1) * G], RB_KK_EPS * RB_KK_EPS))
            q['bonus'] = sums[(n_unit + i) * G:(n_unit + i + 1) * G] * q['v']
            kb = kk * q['a']
            lw, kp = q['lw'], q['kp']
            gs = _sum3(cum_m, lw)
            g = gs[:G]
            gl = gs[G:] if nb > 1 else g[G - 1:G, :]
            eg, eng, egl = jnp.exp(g), jnp.exp(-g), jnp.exp(gl)
            At = -kk * (eg * jnp.exp(-lw))
            Bt = kb * eng
            Kt = kp * eng
            Rt = q['r'] * eg
            dec = egl * eng
            Ke = kp * dec
            Be = kb * dec
            q['tasks'] = []
            for p in range(npair):
                sl = sls[p]
                t = dict(p=p, bs=q['bs'], AR=jnp.concatenate([At[:, sl], Rt[:, sl]], axis=0), B=Bt[:, sl],
                         K=Kt[:, sl], V=q['v'][:, sl], Ke=Ke[:, sl], Be=Be[:, sl], egl=egl[:, sl])
                tasks.append(t)
                q['tasks'].append(t)
        for t in tasks:
            if nb == 1:
                t['S2'] = [s2_s[b, t['p']] for b in t['bs']]
            else:
                t['S'] = [[s0_ref[b, 2 * t['p'] + h] for h in range(2)] for b in t['bs']]
        for t in tasks:
            t['M'] = _dot_nt(t['AR'], jnp.concatenate([stack2(t['B']), stack2(t['K'])], axis=0))
        for t in tasks:
            if nb == 1:
                x0 = _dot_nt(t['AR'], t['S2'][0])
                t['x0a'], t['x0r'] = x0[:G], x0[G:]
            else:
                x0 = []
                for j in range(nb):
                    arj = jnp.concatenate([t['AR'][j * C:(j + 1) * C], t['AR'][G + j * C:G + (j + 1) * C]], axis=0)
                    x0.append(jnp.concatenate([_dot_nt(arj[:, h * RB_N:(h + 1) * RB_N], t['S'][j][h])
                                               for h in range(2)], axis=1))
                t['x0a'] = jnp.concatenate([x[:C] for x in x0], axis=0)
                t['x0r'] = jnp.concatenate([x[C:] for x in x0], axis=0)
        for t in tasks:
            M = t['M']
            lak = jnp.where(tri_s, M[:G, 2 * RB_N:], 0.0)
            lrk = jnp.where(tri_i, M[G:, 2 * RB_N:], 0.0)
            lv = _dot(jnp.concatenate([lak, lrk], axis=0), stack2(t['V']))
            t['X'] = t['x0a'] + lv[:G]
            t['Yv'] = t['x0r'] + lv[G:]
            t['P'] = jnp.where(tri_s, M[:G, :2 * RB_N], 0.0)
        for t in tasks:
            t['T'] = eye2 + t['P']
        for k in range(nsq):
            for t in tasks:
                rhs = stack2(t['P'])
                if k == 0:
                    t['P'] = _dot(t['P'], rhs)
                else:
                    pt = _dot(jnp.concatenate([t['P'], t['T']], axis=0), rhs)
                    t['P'] = pt[:G]
                    t['T'] = t['T'] + pt[G:]
        for t in tasks:
            t['T'] = t['T'] + _dot(t['T'], stack2(t['P']))
        for t in tasks:
            t['SA'] = _dot(t['T'], stack2(t['X']))
        for t in tasks:
            lrb = jnp.where(tri_i, t['M'][G:, :2 * RB_N], 0.0)
            t['y'] = t['Yv'] + _dot(lrb, stack2(t['SA']))
        for t in tasks:
            for j in range(nb):
                js = slice(j * C, (j + 1) * C)
                lhs = jnp.concatenate([t['V'][js], t['SA'][js]], axis=0)
                rhs = jnp.concatenate([t['Ke'][js], t['Be'][js]], axis=0)
                egl_j = t['egl'][j * C:j * C + 1, :]
                if nb == 1:
                    s2_s[t['bs'][j], t['p']] = t['S2'][j] * egl_j + jnp.where(same_head, _dot_tn(lhs, rhs), 0.0)
                else:
                    for h in range(2):
                        hs = slice(h * RB_N, (h + 1) * RB_N)
                        s_ref[t['bs'][j], 2 * t['p'] + h] = t['S'][j][h] * egl_j[:, hs] + _dot_tn(lhs[:, hs], rhs[:, hs])
        y = jnp.concatenate([jnp.concatenate([t['y'] for t in q['tasks']], axis=1) for q in units], axis=0)
        yc = y - _segsum(y, seg) * (1.0 / RB_N)
        var = _segsum(yc * yc, seg) * (1.0 / RB_N)
        yn = yc * lax.rsqrt(var + RB_GN_EPS) * gnw_ref[...] + gnb_ref[...]
        for i, q in enumerate(units):
            out = ((yn[i * G:(i + 1) * G] + q['bonus']) * q['gate']).astype(o_ref.dtype)
            if tl >= G:
                o_ref[q['out_idx']] = out
            else:
                o_ref[q['out_idx']] = out.reshape(nb, tl, RB_W)
        return carry

    lax.fori_loop(0, n_iter, step, 0)

    @pl.when(l == pl.num_programs(1) - 1)
    def _fin():
        sh_ref[...] = pr_ref[:, tl - 1:tl, :]
        if nb == 1:
            for p in range(npair):
                S2 = s2_s[:, p]
                s_ref[:, 2 * p] = S2[:, :RB_N, :RB_N]
                s_ref[:, 2 * p + 1] = S2[:, RB_N:, RB_N:]


def _rwkv(pr, shift0, s0, mu, w0, a0, wwa, g2, k_k, k_a, r_k, gn_w, gn_b, bb, tl, C, out_dtype):
    B, L, _ = pr.shape
    const = lambda b, l: (0, 0)
    tok = lambda b, l: (b, l, 0)
    st = lambda b, l: (b, 0, 0, 0)
    vec = lambda x: x.reshape(1, -1)
    kern = functools.partial(_rwkv_kernel, bb=bb, tl=tl, C=C)
    assert C == RWKV_GROUP or (tl == C and L == tl)
    pair_rows = bb if C == RWKV_GROUP else 1
    return pl.pallas_call(
        kern,
        out_shape=(jax.ShapeDtypeStruct((B, L, RB_W), out_dtype),
                   jax.ShapeDtypeStruct(s0.shape, F32),
                   jax.ShapeDtypeStruct((B, 1, RB_PROJ_W), F32)),
        grid=(B // bb, L // tl),
        in_specs=[pl.BlockSpec((bb, tl, RB_PROJ_W), tok),
                  pl.BlockSpec((bb, SUBLANES, RB_PROJ_W), lambda b, l: (b, jnp.maximum(l * (tl // SUBLANES) - 1, 0), 0)),
                  pl.BlockSpec((bb, 1, RB_PROJ_W), lambda b, l: (b, 0, 0)),
                  pl.BlockSpec((bb, RB_HEADS, RB_N, RB_N), st),
                  pl.BlockSpec((1, RB_PROJ_W), const),
                  pl.BlockSpec((1, RB_W), const),
                  pl.BlockSpec((1, RB_W), const),
                  pl.BlockSpec((RB_LORA_WA, 2 * RB_W), const),
                  pl.BlockSpec((RB_LORA_G, RB_W), const),
                  pl.BlockSpec((1, RB_W), const),
                  pl.BlockSpec((1, RB_W), const),
                  pl.BlockSpec((1, RB_W), const),
                  pl.BlockSpec((1, RB_W), const),
                  pl.BlockSpec((1, RB_W), const)],
        out_specs=(pl.BlockSpec((bb, tl, RB_W), tok),
                   pl.BlockSpec((bb, RB_HEADS, RB_N, RB_N), st),
                   pl.BlockSpec((bb, 1, RB_PROJ_W), lambda b, l: (b, 0, 0))),
        scratch_shapes=[pltpu.VMEM((pair_rows, RB_HEADS // 2, 2 * RB_N, 2 * RB_N), F32)],
        compiler_params=pltpu.CompilerParams(dimension_semantics=("parallel", "arbitrary"),
                                             vmem_limit_bytes=VMEM_LIMIT_BYTES),
        name="rwkv",
    )(pr, pr, shift0.reshape(B, 1, RB_PROJ_W), s0, vec(mu), vec(w0), vec(a0), wwa, g2, vec(k_k), vec(k_a),
      vec(r_k), vec(gn_w), vec(gn_b))


def _mixffn_kernel(oa_ref, ob_ref, pg_ref, x_ref, mod_ref, nf_ref, nfin_ref, wua_ref, wub_ref, wo_ref, wfi_ref,
                   wfo_ref, y_ref, *rest, ns):
    bb, tl, d = x_ref.shape
    if rest:
        wbf, (stage, sem) = rest[:5], rest[5:]

        @pl.when((pl.program_id(0) == 0) & (pl.program_id(1) == 0))
        def _cast():
            _cast_weights(list(zip((wua_ref, wub_ref, wo_ref, wfi_ref, wfo_ref), wbf)), stage, sem)
        wua_ref, wub_ref, wo_ref, wfi_ref, wfo_ref = wbf

    def part(bsl, tsl):
        rows = lambda ref: ref[bsl, tsl, :]
        x = rows(x_ref)
        nb_, nt_ = x.shape[0], x.shape[1]
        R = nb_ * nt_
        mod = lambda i: mod_ref[bsl, i:i + 1, :]
        pg = rows(pg_ref).astype(F32).reshape(R, GATE_W)
        ua = _dot(rows(oa_ref).reshape(R, HA_W), wua_ref[...])
        ub = _dot(rows(ob_ref).reshape(R, RB_W), wub_ref[...])
        yield
        merged = _sigmoid(pg[:, :d]) * ua + _sigmoid(pg[:, d:]) * ub
        mix = _dot(merged, wo_ref[...]).reshape(nb_, nt_, d)
        yield
        x1 = x + mod(2) * mix
        xn = x1 * lax.rsqrt(jnp.mean(x1 * x1, axis=-1, keepdims=True) + RMS_EPS) * nf_ref[...]
        xf = (xn * (1.0 + mod(4)) + mod(3)).reshape(R, d)
        f = None
        for lo, hi in FFN_GROUPS:
            gate = _dot(xf, wfi_ref[:, lo:hi])
            up = _dot(xf, wfi_ref[:, FFN_HIDDEN + lo:FFN_HIDDEN + hi])
            part = _dot(gate * _sigmoid(gate) * up, wfo_ref[lo:hi, :])
            f = part if f is None else f + part
            yield
        f = f.reshape(nb_, nt_, d)
        x2 = x1 + mod(5) * f
        y_ref[bsl, tsl, :] = x2 * lax.rsqrt(jnp.mean(x2 * x2, axis=-1, keepdims=True) + RMS_EPS) * nfin_ref[...]

    _run_staged(_row_parts(part, bb, tl, ns))


def _mixffn(oa, ob, pg, x, mod, mod_off, norm_ffn, norm_final, weights, bb, tl, ns):
    B, L, d = x.shape
    mod_blk = mod_off // bb
    const = lambda b, l: (0, 0)
    tok = lambda b, l: (b, l, 0)
    cast = weights[0].dtype == F32
    out_shape = [jax.ShapeDtypeStruct((B, L, d), F32)]
    out_specs = [pl.BlockSpec((bb, tl, d), tok)]
    if cast:
        w_specs = [pl.BlockSpec(memory_space=pl.ANY) for _ in weights]
        out_shape += [jax.ShapeDtypeStruct(w.shape, BF16) for w in weights]
        out_specs += [pl.BlockSpec(w.shape, const, pipeline_mode=pl.Buffered(1)) for w in weights]
        scratch = [pltpu.VMEM(STAGE_SHAPE, F32), pltpu.SemaphoreType.DMA((STAGE_SHAPE[0],))]
    else:
        w_specs = [pl.BlockSpec(w.shape, const, pipeline_mode=pl.Buffered(1)) for w in weights]
        scratch = []
    return pl.pallas_call(
        functools.partial(_mixffn_kernel, ns=ns),
        out_shape=tuple(out_shape),
        grid=(B // bb, L // tl),
        in_specs=[pl.BlockSpec((bb, tl, HA_W), tok),
                  pl.BlockSpec((bb, tl, RB_W), tok),
                  pl.BlockSpec((bb, tl, GATE_W), tok),
                  pl.BlockSpec((bb, tl, d), tok),
                  pl.BlockSpec((bb, 6, d), lambda b, l: (b + mod_blk, 0, 0)),
                  pl.BlockSpec((1, d), const),
                  pl.BlockSpec((1, d), const)] + w_specs,
        out_specs=tuple(out_specs),
        scratch_shapes=scratch,
        compiler_params=pltpu.CompilerParams(dimension_semantics=("arbitrary", "arbitrary"),
                                             vmem_limit_bytes=VMEM_LIMIT_BYTES),
        name="mixffn",
    )(oa, ob, pg, x, mod, norm_ffn.reshape(1, d), norm_final.reshape(1, d), *weights)


def _block(x, mod, mod_off, s_h, s_r, shift, p, w_in, w_mix, layer, tiles):
    assert mod_off % tiles['inproj'][0] == 0 and mod_off % tiles['mixffn'][0] == 0
    ph, pr, pg, *w_in_bf = _inproj(x, mod, mod_off, p['norm_mix'], w_in, *tiles['inproj'])
    oa, s_h = _hgrn(ph, s_h, p['lower_bounds'], p['hgrn_norm'], layer, *tiles['hgrn'])
    ob, s_r, shift = _rwkv(pr, shift, s_r, p['mu'], p['w0'], p['a0'], p['wwa'], p['g2'], p['k_k'], p['k_a'], p['r_k'],
                           p['gn_w'], p['gn_b'], *tiles['rwkv'])
    y, *w_mix_bf = _mixffn(oa, ob, pg, x, mod, mod_off, p['norm_ffn'], p['norm_final'], w_mix, *tiles['mixffn'])
    return (y, s_h, s_r, shift.reshape(shift.shape[0], RB_PROJ_W)), (w_in_bf[0] if w_in_bf else w_in), (w_mix_bf or w_mix)


PROMPT_TILES = {'inproj': (1, 512, 4), 'mixffn': (1, 512, 2), 'hgrn': (8, 128, 64, 8, BF16), 'rwkv': (4, 256, 64, BF16)}
SAMPLE_TILES = {'inproj': (64, 8, 2), 'mixffn': (64, 8, 2), 'hgrn': (16, 8, 8, 8, F32), 'rwkv': (32, 8, 8, F32)}


def kernel(x_prompt, x_sample, c_prompt, c_sample, state_hgrn, state_rwkv, state_shift, w_ada, b_ada, norm_mix, norm_ffn, w_in, hgrn_lower_bounds, hgrn_norm, rwkv_mu, rwkv_w0, rwkv_w2, rwkv_a0, rwkv_a2, rwkv_g2, rwkv_k_k, rwkv_k_a, rwkv_r_k, rwkv_gn_w, rwkv_gn_b, w_up_a, w_up_b, w_out, w_ffn_in, w_ffn_out, norm_final):
    depth = w_ada.shape[0]
    assert depth == 1, "the final norm is fused into the layer kernel"
    nbp, nbs = x_prompt.shape[0], x_sample.shape[0]

    hp = jnp.zeros((nbp, HA_HEADS, HA_DK, HA_DK), F32)
    rp = jnp.zeros((nbp, RB_HEADS, RB_N, RB_N), F32)
    sp = jnp.zeros((nbp, RB_PROJ_W), x_prompt.dtype)

    yp, ys = x_prompt, x_sample
    outs = [[] for _ in range(6)]
    for l in range(depth):
        half = RB_LORA_WA // 2
        zero = jnp.zeros((half, RB_W), F32)
        wwa = jnp.concatenate([jnp.concatenate([rwkv_w2[l], zero], axis=1),
                               jnp.concatenate([zero, rwkv_a2[l]], axis=1)], axis=0)
        p = {
            'norm_mix': norm_mix[l], 'norm_ffn': norm_ffn[l], 'norm_final': norm_final,
            'lower_bounds': hgrn_lower_bounds, 'hgrn_norm': hgrn_norm[l],
            'mu': rwkv_mu[l], 'w0': rwkv_w0[l], 'a0': rwkv_a0[l], 'wwa': wwa.astype(BF16),
            'g2': rwkv_g2[l].astype(BF16), 'k_k': rwkv_k_k[l], 'k_a': rwkv_k_a[l], 'r_k': rwkv_r_k[l],
            'gn_w': rwkv_gn_w[l], 'gn_b': rwkv_gn_b[l],
        }
        w_mix = (w_up_a[l], w_up_b[l], w_out[l], w_ffn_in[l], w_ffn_out[l])
        mod = _ada(jnp.concatenate([c_sample, c_prompt], axis=0), w_ada[l], b_ada[l])
        (yp, h1, r1, s1), w_in_bf, w_mix_bf = _block(yp, mod, nbs, hp, rp, sp, p, w_in[l], w_mix, l, PROMPT_TILES)
        (ys, h2, r2, s2), _, _ = _block(ys, mod, 0, state_hgrn[l], state_rwkv[l], state_shift[l], p, w_in_bf,
                                        tuple(w_mix_bf), l, SAMPLE_TILES)
        for lst, val in zip(outs, (h1, r1, s1, h2, r2, s2)):
            lst.append(val)
    stacked = [jnp.stack(o) for o in outs]
    return (yp, ys, *stacked)
```
